```python
import math
import jax, jax.numpy as jnp
from jax import lax
import numpy as np


D_MODEL = 1024
BATCH = 32
SEQ = 2048
DEPTH = 1
DEC_BATCH = 4
DEC_SEQ = 8192
PAST_LEN = 128

GRID_W = 64
Q_BLOCK = 128
EPS = 1e-6
HA_Q = 8
HA_KV = 2
G_A = HA_Q // HA_KV
HD_A = 128
ROPE_AXIS_DIM = HD_A // 2
ROPE_THETA = 10000.0
HB = 8
DH_B = 64
DV_B = 2 * DH_B
N_BUCKETS = 32
MAX_DIST = 128
D_FF = 2816
CONV_W = 3
WA_Q = HA_Q * HD_A
WA_KV = HA_KV * HD_A
WB_QK = HB * 2 * DH_B
WB_V = HB * DV_B
N_IN = WA_Q + 2 * WA_KV + 2 * WB_QK + WB_V + 2 * D_MODEL

kernel_name = "hybrid_gqa_diffattn_convffn_encoder"


def rmsnorm(x, g):
    xf = x.astype(jnp.float32)
    y = xf * lax.rsqrt(jnp.mean(xf * xf, axis=-1, keepdims=True) + EPS)
    return (y * g.astype(jnp.float32)).astype(x.dtype)


def lambda_init(l):
    return 0.8 - 0.6 * math.exp(-0.3 * l)


def axial_rope_tables(n):
    rows = n // GRID_W
    row = jnp.repeat(jnp.arange(rows, dtype=jnp.float32), GRID_W)
    col = jnp.tile(jnp.arange(GRID_W, dtype=jnp.float32), rows)
    inv = ROPE_THETA ** (-jnp.arange(0, ROPE_AXIS_DIM, 2, dtype=jnp.float32) / ROPE_AXIS_DIM)
    ang_r = row[:, None] * inv[None, :]
    ang_c = col[:, None] * inv[None, :]
    return jnp.cos(ang_r), jnp.sin(ang_r), jnp.cos(ang_c), jnp.sin(ang_c)


def _rotate(x, cos, sin):
    half = x.shape[-1] // 2
    x1, x2 = x[..., :half], x[..., half:]
    cos = cos[:, None, :].astype(x.dtype)
    sin = sin[:, None, :].astype(x.dtype)
    return jnp.concatenate([x1 * cos - x2 * sin, x2 * cos + x1 * sin], axis=-1)


def apply_axial_rope(x, tabs):
    cr, sr, cc, sc = tabs
    return jnp.concatenate([_rotate(x[..., :ROPE_AXIS_DIM], cr, sr),
                            _rotate(x[..., ROPE_AXIS_DIM:], cc, sc)], axis=-1)


def t5_bucket(rel):
    nb = N_BUCKETS // 2
    ret = (rel > 0).astype(jnp.int32) * nb
    n = jnp.abs(rel)
    max_exact = nb // 2
    large = max_exact + (jnp.log(jnp.maximum(n, 1).astype(jnp.float32) / max_exact)
                         / math.log(MAX_DIST / max_exact) * (nb - max_exact)).astype(jnp.int32)
    large = jnp.minimum(large, nb - 1)
    return ret + jnp.where(n < max_exact, n, large)


def to_blocks(t):
    b, n = t.shape[:2]
    return jnp.moveaxis(t.reshape((b, n // Q_BLOCK, Q_BLOCK) + t.shape[2:]), 1, 0)


def from_blocks(t):
    t = jnp.moveaxis(t, 0, 1)
    return t.reshape((t.shape[0], t.shape[1] * t.shape[2]) + t.shape[3:])


def gqa_mixer(q, k, v):
    scale = HD_A ** -0.5

    def block(qb):
        s = jnp.einsum('bqkgd,bskd->bkgqs', qb, k).astype(jnp.float32) * scale
        p = jax.nn.softmax(s, axis=-1).astype(v.dtype)
        return jnp.einsum('bkgqs,bskd->bqkgd', p, v)

    return from_blocks(lax.map(block, to_blocks(q)))


def diff_mixer(q1, q2, k1, k2, v, lam, rel_bias):
    n = k1.shape[1]
    scale = DH_B ** -0.5
    kpos = jnp.arange(n, dtype=jnp.int32)
    qpos = kpos.reshape(n // Q_BLOCK, Q_BLOCK)

    def block(args):
        q1b, q2b, qp = args
        rel = kpos[None, :] - qp[:, None]
        bias = jnp.moveaxis(rel_bias[t5_bucket(rel)], -1, 0).astype(jnp.float32)
        s1 = jnp.einsum('bqhd,bshd->bhqs', q1b, k1).astype(jnp.float32) * scale + bias
        s2 = jnp.einsum('bqhd,bshd->bhqs', q2b, k2).astype(jnp.float32) * scale + bias
        a = (jax.nn.softmax(s1, axis=-1) - lam * jax.nn.softmax(s2, axis=-1)).astype(v.dtype)
        return jnp.einsum('bhqs,bshe->bqhe', a, v)

    return from_blocks(lax.map(block, (to_blocks(q1), to_blocks(q2), qpos)))


def depthwise_conv(g, w, b):
    n = g.shape[1]
    pad = CONV_W // 2
    gp = jnp.pad(g, ((0, 0), (pad, CONV_W - 1 - pad), (0, 0)))
    z = b
    for j in range(CONV_W):
        z = z + gp[:, j:j + n] * w[j]
    return z


def encoder_layer(x, c, l, tabs, p):
    b, n, _ = x.shape
    mod = jnp.einsum('bd,de->be', jax.nn.silu(c), p['w_mod'][l]) + p['b_mod'][l]
    sh1, sc1, gt1, sh2, sc2, gt2 = [m[:, None, :] for m in jnp.split(mod, 6, axis=-1)]

    h = rmsnorm(x, p['g_norm1'][l]) * (1 + sc1) + sh1
    proj = jnp.einsum('bnd,de->bne', h, p['w_in'][l])
    widths = [WA_Q, WA_KV, WA_KV, WB_QK, WB_QK, WB_V, D_MODEL, D_MODEL]
    idx = [int(s) for s in np.cumsum(widths)[:-1]]
    qa, ka, va, qb, kb, vb, ga, gb = jnp.split(proj, idx, axis=-1)

    qa = apply_axial_rope(rmsnorm(qa.reshape(b, n, HA_Q, HD_A), p['g_qnorm'][l]), tabs)
    ka = apply_axial_rope(rmsnorm(ka.reshape(b, n, HA_KV, HD_A), p['g_knorm'][l]), tabs)
    va = va.reshape(b, n, HA_KV, HD_A)
    o_a = gqa_mixer(qa.reshape(b, n, HA_KV, G_A, HD_A), ka, va).reshape(b, n, WA_Q)

    qb = qb.reshape(b, n, HB, 2, DH_B)
    kb = kb.reshape(b, n, HB, 2, DH_B)
    vb = vb.reshape(b, n, HB, DV_B)
    f32 = jnp.float32
    lam = (jnp.exp(jnp.sum(p['lambda_q1'][l].astype(f32) * p['lambda_k1'][l].astype(f32)))
           - jnp.exp(jnp.sum(p['lambda_q2'][l].astype(f32) * p['lambda_k2'][l].astype(f32)))
           + lambda_init(l))
    o_b = diff_mixer(qb[..., 0, :], qb[..., 1, :], kb[..., 0, :], kb[..., 1, :], vb, lam, p['rel_bias'])
    o_b = (rmsnorm(o_b, p['g_subln'][l]) * (1.0 - lambda_init(l))).reshape(b, n, WB_V)

    merged = jax.nn.sigmoid(ga) * o_a + jax.nn.sigmoid(gb) * o_b
    x = x + gt1 * jnp.einsum('bnd,de->bne', merged, p['w_out'][l])

    h2 = rmsnorm(x, p['g_norm2'][l]) * (1 + sc2) + sh2
    u, g = jnp.split(jnp.einsum('bnd,df->bnf', h2, p['w_ffn_in'][l]), 2, axis=-1)
    a = jax.nn.gelu(depthwise_conv(g, p['conv_w'][l], p['conv_b'][l]), approximate=False) * u
    x = x + gt2 * jnp.einsum('bnf,fd->bnd', a, p['w_down'][l])
    return x


def encoder_trunk(x, c, p):
    tabs = axial_rope_tables(x.shape[1])
    for l in range(DEPTH):
        x = encoder_layer(x, c, l, tabs, p)
    return rmsnorm(x, p['g_final'])


def setup_inputs(seed: int = 0) -> dict:
    key = jax.random.key(seed)
    ks = jax.random.split(key, 24)
    f32 = jnp.float32
    nrm = lambda k, shape, s: jax.random.normal(k, shape, f32) * s
    gain = lambda k, shape: 1.0 + 0.01 * jax.random.normal(k, shape, f32)
    D = D_MODEL
    return {
        'x_prompt': nrm(ks[0], (BATCH, SEQ, D), 1.0),
        'x_sample': nrm(ks[1], (DEC_BATCH, DEC_SEQ, D), 1.0),
        'c_prompt': nrm(ks[2], (BATCH, D), 1.0),
        'c_sample': nrm(ks[3], (DEC_BATCH, D), 1.0),
        'w_mod': nrm(ks[4], (DEPTH, D, 6 * D), 0.5 * D ** -0.5),
        'b_mod': nrm(ks[5], (DEPTH, 6 * D), 0.01),
        'g_norm1': gain(ks[6], (DEPTH, D)),
        'w_in': nrm(ks[7], (DEPTH, D, N_IN), D ** -0.5),
        'g_qnorm': gain(ks[8], (DEPTH, HD_A)),
        'g_knorm': gain(ks[9], (DEPTH, HD_A)),
        'lambda_q1': nrm(ks[10], (DEPTH, DH_B), 0.1),
        'lambda_k1': nrm(ks[11], (DEPTH, DH_B), 0.1),
        'lambda_q2': nrm(ks[12], (DEPTH, DH_B), 0.1),
        'lambda_k2': nrm(ks[13], (DEPTH, DH_B), 0.1),
        'g_subln': gain(ks[14], (DEPTH, DV_B)),
        'rel_bias': nrm(ks[15], (N_BUCKETS, HB), 0.5),
        'w_out': nrm(ks[16], (DEPTH, D, D), D ** -0.5),
        'g_norm2': gain(ks[17], (DEPTH, D)),
        'w_ffn_in': nrm(ks[18], (DEPTH, D, 2 * D_FF), D ** -0.5),
        'conv_w': nrm(ks[19], (DEPTH, CONV_W, D_FF), CONV_W ** -0.5),
        'conv_b': nrm(ks[20], (DEPTH, D_FF), 0.01),
        'w_down': nrm(ks[21], (DEPTH, D_FF, D), D_FF ** -0.5),
        'g_final': gain(ks[22], (D,)),
    }


def reference(x_prompt, x_sample, c_prompt, c_sample, w_mod, b_mod, g_norm1, w_in, g_qnorm, g_knorm,
              lambda_q1, lambda_k1, lambda_q2, lambda_k2, g_subln, rel_bias, w_out, g_norm2,
              w_ffn_in, conv_w, conv_b, w_down, g_final):
    params = {
        'w_mod': w_mod, 'b_mod': b_mod, 'g_norm1': g_norm1, 'w_in': w_in,
        'g_qnorm': g_qnorm, 'g_knorm': g_knorm,
        'lambda_q1': lambda_q1, 'lambda_k1': lambda_k1, 'lambda_q2': lambda_q2, 'lambda_k2': lambda_k2,
        'g_subln': g_subln, 'rel_bias': rel_bias, 'w_out': w_out, 'g_norm2': g_norm2,
        'w_ffn_in': w_ffn_in, 'conv_w': conv_w, 'conv_b': conv_b, 'w_down': w_down, 'g_final': g_final,
    }
    y_prompt = encoder_trunk(x_prompt, c_prompt, params)
    y_sample = encoder_trunk(x_sample, c_sample, params)
    return (y_prompt, y_sample)
```

```python
import functools
import math

import jax
import jax.numpy as jnp
from jax import lax
from jax.experimental import pallas as pl
from jax.experimental.pallas import tpu as pltpu

F32 = jnp.float32
BF16 = jnp.bfloat16

D_MODEL = 1024
GRID_W = 64
EPS = 1e-6
HA_Q, HA_KV, G_A, HD_A = 8, 2, 4, 128
ROPE_AXIS_DIM = HD_A // 2
ROPE_THETA = 10000.0
HB, DH_B, DV_B = 8, 64, 128
N_BUCKETS, MAX_DIST = 32, 128
D_FF = 2816
CONV_W = 3
LAMBDA_INIT = 0.8 - 0.6 * math.exp(-0.3 * 0)
LOG2E = 1.4426950408889634
SCALE_A = HD_A ** -0.5 * LOG2E
SCALE_B = DH_B ** -0.5 * LOG2E
NEG_BIG = -0.7 * float(jnp.finfo(jnp.float32).max)

OFF_QA, OFF_KA, OFF_VA = 0, 1024, 1280
OFF_QB, OFF_KB, OFF_VB = 1536, 2560, 3584
OFF_GA, OFF_GB = 4608, 5632
N_IN = 6656

LANES = 128
VMEM_LIMIT = 56 * 1024 * 1024

TM_IN = 512
TM_MID = 256
TM_FFN = 256
T_A = 512
T_B = 512


def _cparams(sem):
    return pltpu.CompilerParams(dimension_semantics=sem, vmem_limit_bytes=VMEM_LIMIT)


def _const_spec(shape):
    nd = len(shape)
    return pl.BlockSpec(shape, lambda *_: (0,) * nd, pipeline_mode=pl.Buffered(1))


def _mod_kernel(c_ref, w_ref, b_ref, o_ref):
    c = c_ref[...]
    sc = c * jax.nn.sigmoid(c)
    o_ref[...] = jnp.dot(sc, w_ref[...], preferred_element_type=F32,
                         precision=lax.Precision.HIGHEST) + b_ref[...]


def _mod(c, w_mod, b_mod):
    rows = c.shape[0]
    n_out = w_mod.shape[1]
    blk = 1024
    return pl.pallas_call(
        _mod_kernel,
        grid=(n_out // blk,),
        in_specs=[pl.BlockSpec((rows, D_MODEL), lambda j: (0, 0)),
                  pl.BlockSpec((D_MODEL, blk), lambda j: (0, j)),
                  pl.BlockSpec((1, blk), lambda j: (0, j))],
        out_specs=pl.BlockSpec((rows, blk), lambda j: (0, j)),
        out_shape=jax.ShapeDtypeStruct((rows, n_out), F32),
        compiler_params=_cparams(("parallel",)),
        name="mod",
    )(c, w_mod, b_mod)


def _bias_kernel(tab_ref, o_ref, *, t):
    cls = pl.program_id(0)
    row = lax.broadcasted_iota(jnp.int32, (t, t), 0)
    col = lax.broadcasted_iota(jnp.int32, (t, t), 1)
    rel = (cls - 1) * t + col - row
    nb = N_BUCKETS // 2
    max_exact = nb // 2
    ret = jnp.where(rel > 0, nb, 0)
    n = jnp.abs(rel)
    large = max_exact + (jnp.log(jnp.maximum(n, 1).astype(F32) / max_exact)
                         / math.log(MAX_DIST / max_exact) * (nb - max_exact)).astype(jnp.int32)
    large = jnp.minimum(large, nb - 1)
    bucket = ret + jnp.where(n < max_exact, n, large)
    for h in range(HB):
        acc = jnp.zeros((t, t), F32)
        for b in range(N_BUCKETS):
            acc = jnp.where(bucket == b, tab_ref[b, h], acc)
        o_ref[h, 0] = acc * LOG2E


def _bias_tiles(rel_bias, t):
    return pl.pallas_call(
        functools.partial(_bias_kernel, t=t),
        grid=(3,),
        in_specs=[pl.BlockSpec(memory_space=pltpu.SMEM)],
        out_specs=pl.BlockSpec((HB, 1, t, t), lambda c: (0, c, 0, 0)),
        out_shape=jax.ShapeDtypeStruct((HB, 3, t, t), F32),
        compiler_params=_cparams(("parallel",)),
        name="bias_tiles",
    )(rel_bias)


def _inproj_kernel(x_ref, mod_ref, g1_ref, gq_ref, gk_ref, cos_ref, sin_ref, w_ref,
                   qa_ref, ka_ref, va_ref, qb_ref, kb_ref, vb_ref, ga_ref, gb_ref, *, tm):
    x = x_ref[0]
    y = x * lax.rsqrt(jnp.mean(x * x, axis=-1, keepdims=True) + EPS)
    sh1 = mod_ref[0, 0:1, :]
    sc1 = mod_ref[0, 1:2, :]
    hb = ((y * g1_ref[...]) * (1.0 + sc1) + sh1).astype(BF16)

    cos = cos_ref[...]
    sin = sin_ref[...]
    lane = lax.broadcasted_iota(jnp.int32, (tm, LANES), 1)
    low_half = (lane & 32) == 0

    def proj(c0, width):
        return jnp.dot(hb, w_ref[:, c0:c0 + width], preferred_element_type=F32)

    def norm_rope(p, g, scale):
        yn = p * lax.rsqrt(jnp.mean(p * p, axis=-1, keepdims=True) + EPS) * g
        partner = jnp.where(low_half, pltpu.roll(yn, 96, 1), pltpu.roll(yn, 32, 1))
        r = yn * cos + partner * sin
        return r * scale if scale is not None else r

    gq = gq_ref[...]
    gk = gk_ref[...]
    for c in range(2):
        p = proj(OFF_QA + c * 512, 512)
        for j in range(4):
            qa_ref[0, c * 4 + j] = norm_rope(p[:, j * LANES:(j + 1) * LANES], gq, SCALE_A).astype(BF16)
    p = proj(OFF_KA, 512)
    for j in range(2):
        ka_ref[0, j] = norm_rope(p[:, j * LANES:(j + 1) * LANES], gk, None).astype(BF16)
        va_ref[0, j] = p[:, (2 + j) * LANES:(3 + j) * LANES].astype(BF16)
    for c in range(2):
        p = proj(OFF_QB + c * 512, 512)
        for j in range(4):
            qb_ref[0, c * 4 + j] = (p[:, j * LANES:(j + 1) * LANES] * SCALE_B).astype(BF16)
    for c in range(2):
        p = proj(OFF_KB + c * 512, 512)
        for j in range(4):
            kb_ref[0, c * 4 + j] = p[:, j * LANES:(j + 1) * LANES].astype(BF16)
    for c in range(2):
        p = proj(OFF_VB + c * 512, 512)
        for j in range(4):
            vb_ref[0, c * 4 + j] = p[:, j * LANES:(j + 1) * LANES].astype(BF16)
    for c in range(2):
        ga_ref[0, :, c * 512:(c + 1) * 512] = proj(OFF_GA + c * 512, 512)
    for c in range(2):
        gb_ref[0, :, c * 512:(c + 1) * 512] = proj(OFF_GB + c * 512, 512)


def _inproj(x, mod3, g1, gq, gk, cos_t, sin_t, w_in):
    b, n, _ = x.shape
    tm = TM_IN
    hm = lambda heads: pl.BlockSpec((1, heads, tm, LANES), lambda bi, i: (bi, 0, i, 0))
    tok = pl.BlockSpec((1, tm, D_MODEL), lambda bi, i: (bi, i, 0))
    hshape = lambda heads: jax.ShapeDtypeStruct((b, heads, n, LANES), BF16)
    return pl.pallas_call(
        functools.partial(_inproj_kernel, tm=tm),
        grid=(b, n // tm),
        in_specs=[tok,
                  pl.BlockSpec((1, 6, D_MODEL), lambda bi, i: (bi, 0, 0)),
                  _const_spec((1, D_MODEL)), _const_spec((1, LANES)), _const_spec((1, LANES)),
                  pl.BlockSpec((tm, LANES), lambda bi, i: (i, 0)),
                  pl.BlockSpec((tm, LANES), lambda bi, i: (i, 0)),
                  _const_spec((D_MODEL, N_IN))],
        out_specs=[hm(HA_Q), hm(HA_KV), hm(HA_KV), hm(HB), hm(HB), hm(HB), tok, tok],
        out_shape=[hshape(HA_Q), hshape(HA_KV), hshape(HA_KV), hshape(HB), hshape(HB), hshape(HB),
                   jax.ShapeDtypeStruct((b, n, D_MODEL), F32), jax.ShapeDtypeStruct((b, n, D_MODEL), F32)],
        compiler_params=_cparams(("parallel", "parallel")),
        name="inproj",
    )(x, mod3, g1, gq, gk, cos_t, sin_t, w_in)


def _softmax_step(s, sub_shift, m_prev, l_prev, acc_prev, v, tk):
    m_cur = jnp.max(s, axis=1, keepdims=True)
    if sub_shift is not None:
        m_cur = m_cur + sub_shift
    m_next = jnp.maximum(m_prev, m_cur)
    sub = m_next if sub_shift is None else m_next - sub_shift
    p = jnp.exp2(s - pltpu.repeat(sub, tk // LANES, 1))
    alpha = jnp.exp2(m_prev - m_next)
    l_next = alpha * l_prev + jnp.sum(p, axis=1, keepdims=True)
    acc_next = acc_prev * alpha + jnp.dot(p.astype(BF16), v, preferred_element_type=F32)
    return m_next, l_next, acc_next


def _gqa_kernel(q_ref, k_ref, v_ref, o_ref, m_sc, l_sc, acc_sc, *, tq, tk, nk):
    ki = pl.program_id(2)

    @pl.when(ki == 0)
    def _():
        m_sc[...] = jnp.full(m_sc.shape, NEG_BIG, F32)
        l_sc[...] = jnp.zeros(l_sc.shape, F32)
        acc_sc[...] = jnp.zeros(acc_sc.shape, F32)

    for kv in range(HA_KV):
        q = q_ref[0, kv * G_A:(kv + 1) * G_A].reshape(G_A * tq, HD_A)
        s = lax.dot_general(q, k_ref[0, kv], (((1,), (1,)), ((), ())), preferred_element_type=F32)
        m, l, acc = _softmax_step(s, None, m_sc[kv], l_sc[kv], acc_sc[kv], v_ref[0, kv], tk)
        m_sc[kv] = m
        l_sc[kv] = l
        acc_sc[kv] = acc

    @pl.when(ki == nk - 1)
    def _():
        for kv in range(HA_KV):
            o = acc_sc[kv] / l_sc[kv]
            for g in range(G_A):
                h = kv * G_A + g
                o_ref[0, :, h * HD_A:(h + 1) * HD_A] = o[g * tq:(g + 1) * tq]


def _gqa(qa, ka, va):
    b, _, n, _ = qa.shape
    tq = tk = T_A
    nq, nk = n // tq, n // tk
    rows = G_A * tq
    return pl.pallas_call(
        functools.partial(_gqa_kernel, tq=tq, tk=tk, nk=nk),
        grid=(b, nq, nk),
        in_specs=[pl.BlockSpec((1, HA_Q, tq, HD_A), lambda bi, qi, ki: (bi, 0, qi, 0)),
                  pl.BlockSpec((1, HA_KV, tk, HD_A), lambda bi, qi, ki: (bi, 0, ki, 0)),
                  pl.BlockSpec((1, HA_KV, tk, HD_A), lambda bi, qi, ki: (bi, 0, ki, 0))],
        out_specs=pl.BlockSpec((1, tq, D_MODEL), lambda bi, qi, ki: (bi, qi, 0)),
        out_shape=jax.ShapeDtypeStruct((b, n, D_MODEL), F32),
        scratch_shapes=[pltpu.VMEM((HA_KV, rows, LANES), F32),
                        pltpu.VMEM((HA_KV, rows, LANES), F32),
                        pltpu.VMEM((HA_KV, rows, HD_A), F32)],
        compiler_params=_cparams(("parallel", "parallel", "arbitrary")),
        name="gqa",
    )(qa, ka, va)


def _diff_kernel(cb_ref, lam_ref, gs_ref, q_ref, k_ref, v_ref, bias_ref, o_ref,
                 q2_sc, m_sc, l_sc, acc_sc, *, t, nk):
    qi = pl.program_id(1)
    ki = pl.program_id(2)

    @pl.when(ki == 0)
    def _():
        m_sc[...] = jnp.full(m_sc.shape, NEG_BIG, F32)
        l_sc[...] = jnp.zeros(l_sc.shape, F32)
        acc_sc[...] = jnp.zeros(acc_sc.shape, F32)
        first = lax.broadcasted_iota(jnp.int32, (t, LANES), 1) < DH_B
        zero = jnp.zeros((t, LANES), BF16)
        for h in range(HB):
            q = q_ref[0, h]
            q2_sc[h, :t] = jnp.where(first, q, zero)
            q2_sc[h, t:] = jnp.where(first, zero, q)

    def head_step(h, near):
        s = lax.dot_general(q2_sc[h], k_ref[0, h], (((1,), (1,)), ((), ())), preferred_element_type=F32)
        if near:
            s = (s.reshape(2, t, t) + bias_ref[h, 0][None]).reshape(2 * t, t)
            shift = None
        else:
            shift = jnp.where(ki > qi, cb_ref[h, 1], cb_ref[h, 0])
        m, l, acc = _softmax_step(s, shift, m_sc[h], l_sc[h], acc_sc[h], v_ref[0, h], t)
        m_sc[h] = m
        l_sc[h] = l
        acc_sc[h] = acc

    near_diag = jnp.abs(ki - qi) <= 1

    @pl.when(near_diag)
    def _():
        def body(h, c):
            head_step(h, True)
            return c
        lax.fori_loop(0, HB, body, 0)

    @pl.when(jnp.logical_not(near_diag))
    def _():
        def body(h, c):
            head_step(h, False)
            return c
        lax.fori_loop(0, HB, body, 0)

    @pl.when(ki == nk - 1)
    def _():
        lam = (jnp.exp(jnp.sum(lam_ref[0:1, :] * lam_ref[1:2, :], axis=-1, keepdims=True))
               - jnp.exp(jnp.sum(lam_ref[2:3, :] * lam_ref[3:4, :], axis=-1, keepdims=True))
               + LAMBDA_INIT)
        gs = gs_ref[...]
        for h in range(HB):
            o12 = acc_sc[h] / l_sc[h]
            o = o12[:t] - lam * o12[t:]
            o = o * lax.rsqrt(jnp.mean(o * o, axis=-1, keepdims=True) + EPS) * gs
            o_ref[0, :, h * DV_B:(h + 1) * DV_B] = o * (1.0 - LAMBDA_INIT)


def _diff(qb, kb, vb, bias_t, cb, lam4, gs):
    b, _, n, _ = qb.shape
    t = T_B
    nq = nk = n // t
    hm = lambda idx: pl.BlockSpec((1, HB, t, LANES), idx)
    return pl.pallas_call(
        functools.partial(_diff_kernel, t=t, nk=nk),
        grid=(b, nq, nk),
        in_specs=[pl.BlockSpec(memory_space=pltpu.SMEM),
                  _const_spec((4, DH_B)), _const_spec((1, DV_B)),
                  hm(lambda bi, qi, ki: (bi, 0, qi, 0)),
                  hm(lambda bi, qi, ki: (bi, 0, ki, 0)),
                  hm(lambda bi, qi, ki: (bi, 0, ki, 0)),
                  pl.BlockSpec((HB, 1, t, t), lambda bi, qi, ki: (0, jnp.clip(ki - qi, -1, 1) + 1, 0, 0))],
        out_specs=pl.BlockSpec((1, t, D_MODEL), lambda bi, qi, ki: (bi, qi, 0)),
        out_shape=jax.ShapeDtypeStruct((b, n, D_MODEL), F32),
        scratch_shapes=[pltpu.VMEM((HB, 2 * t, LANES), BF16),
                        pltpu.VMEM((HB, 2 * t, LANES), F32),
                        pltpu.VMEM((HB, 2 * t, LANES), F32),
                        pltpu.VMEM((HB, 2 * t, DV_B), F32)],
        compiler_params=_cparams(("parallel", "parallel", "arbitrary")),
        name="diff",
    )(cb, lam4, gs, qb, kb, vb, bias_t)


def _mid_kernel(x_ref, oa_ref, ob_ref, ga_ref, gb_ref, mod_ref, g2_ref, wo_ref, wf_ref,
                x1_ref, u_ref, g_ref):
    merged = (jax.nn.sigmoid(ga_ref[0]) * oa_ref[0] + jax.nn.sigmoid(gb_ref[0]) * ob_ref[0]).astype(BF16)
    gt1 = mod_ref[0, 2:3, :]
    sh2 = mod_ref[0, 3:4, :]
    sc2 = mod_ref[0, 4:5, :]
    x1 = x_ref[0] + gt1 * jnp.dot(merged, wo_ref[...], preferred_element_type=F32)
    x1_ref[0] = x1
    y = x1 * lax.rsqrt(jnp.mean(x1 * x1, axis=-1, keepdims=True) + EPS)
    h2 = ((y * g2_ref[...]) * (1.0 + sc2) + sh2).astype(BF16)
    cw = 256
    for c in range(D_FF // cw):
        u_ref[0, :, c * cw:(c + 1) * cw] = jnp.dot(
            h2, wf_ref[:, c * cw:(c + 1) * cw], preferred_element_type=F32).astype(BF16)
        g_ref[0, :, c * cw:(c + 1) * cw] = jnp.dot(
            h2, wf_ref[:, D_FF + c * cw:D_FF + (c + 1) * cw], preferred_element_type=F32)


def _mid(x, oa, ob, ga, gb, mod3, g2, w_out, w_ffn_in):
    b, n, _ = x.shape
    tm = TM_MID
    tok = pl.BlockSpec((1, tm, D_MODEL), lambda bi, i: (bi, i, 0))
    ff = pl.BlockSpec((1, tm, D_FF), lambda bi, i: (bi, i, 0))
    return pl.pallas_call(
        _mid_kernel,
        grid=(b, n // tm),
        in_specs=[tok, tok, tok, tok, tok,
                  pl.BlockSpec((1, 6, D_MODEL), lambda bi, i: (bi, 0, 0)),
                  _const_spec((1, D_MODEL)),
                  _const_spec((D_MODEL, D_MODEL)),
                  _const_spec((D_MODEL, 2 * D_FF))],
        out_specs=[tok, ff, ff],
        out_shape=[jax.ShapeDtypeStruct((b, n, D_MODEL), F32),
                   jax.ShapeDtypeStruct((b, n, D_FF), BF16),
                   jax.ShapeDtypeStruct((b, n, D_FF), F32)],
        compiler_params=_cparams(("parallel", "parallel")),
        name="mid",
    )(x, oa, ob, ga, gb, mod3, g2, w_out, w_ffn_in)


def _ffn_out_kernel(x1_ref, u_ref, g_ref, gp_ref, gn_ref, cw_ref, cbias_ref, mod_ref, gf_ref, wd_ref,
                    y_ref, gs_sc, *, tm, nt):
    i = pl.program_id(1)
    g = g_ref[0]
    gs_sc[8:8 + tm, :] = g
    gs_sc[7:8, :] = jnp.where(i > 0, gp_ref[0, 7:8, :], 0.0)
    gs_sc[8 + tm:9 + tm, :] = jnp.where(i < nt - 1, gn_ref[0, 0:1, :], 0.0)
    z = (cbias_ref[...] + gs_sc[7:7 + tm, :] * cw_ref[0:1, :] + g * cw_ref[1:2, :]
         + gs_sc[9:9 + tm, :] * cw_ref[2:3, :])
    gelu = 0.5 * z * (1.0 + lax.erf(z * math.sqrt(0.5)))
    a = (gelu * u_ref[0].astype(F32)).astype(BF16)
    gt2 = mod_ref[0, 5:6, :]
    x2 = x1_ref[0] + gt2 * jnp.dot(a, wd_ref[...], preferred_element_type=F32)
    y_ref[0] = x2 * lax.rsqrt(jnp.mean(x2 * x2, axis=-1, keepdims=True) + EPS) * gf_ref[...]


def _ffn_out(x1, u, g, conv_w, conv_b, mod3, g_final, w_down):
    b, n, _ = x1.shape
    tm = TM_FFN
    nt = n // tm
    r8 = tm // 8
    tok = pl.BlockSpec((1, tm, D_MODEL), lambda bi, i: (bi, i, 0))
    ff = pl.BlockSpec((1, tm, D_FF), lambda bi, i: (bi, i, 0))
    return pl.pallas_call(
        functools.partial(_ffn_out_kernel, tm=tm, nt=nt),
        grid=(b, nt),
        in_specs=[tok, ff, ff,
                  pl.BlockSpec((1, 8, D_FF), lambda bi, i: (bi, jnp.maximum(i * r8 - 1, 0), 0)),
                  pl.BlockSpec((1, 8, D_FF), lambda bi, i: (bi, jnp.minimum((i + 1) * r8, n // 8 - 1), 0)),
                  _const_spec((CONV_W, D_FF)), _const_spec((1, D_FF)),
                  pl.BlockSpec((1, 6, D_MODEL), lambda bi, i: (bi, 0, 0)),
                  _const_spec((1, D_MODEL)),
                  _const_spec((D_FF, D_MODEL))],
        out_specs=tok,
        out_shape=jax.ShapeDtypeStruct((b, n, D_MODEL), F32),
        scratch_shapes=[pltpu.VMEM((tm + 16, D_FF), F32)],
        compiler_params=_cparams(("parallel", "parallel")),
        name="ffn_out",
    )(x1, u, g, g, g, conv_w, conv_b, mod3, g_final, w_down)


def _rope_tables(n):
    t = jnp.arange(n, dtype=jnp.int32)
    row = (t // GRID_W).astype(F32)
    col = (t % GRID_W).astype(F32)
    inv = ROPE_THETA ** (-jnp.arange(0, ROPE_AXIS_DIM, 2, dtype=F32) / ROPE_AXIS_DIM)
    ar = row[:, None] * inv[None, :]
    ac = col[:, None] * inv[None, :]
    cos_t = jnp.concatenate([jnp.cos(ar), jnp.cos(ar), jnp.cos(ac), jnp.cos(ac)], axis=-1)
    sin_t = jnp.concatenate([-jnp.sin(ar), jnp.sin(ar), -jnp.sin(ac), jnp.sin(ac)], axis=-1)
    return cos_t, sin_t


def _trunk(x, mod, p):
    b, n, _ = x.shape
    mod3 = mod.reshape(b, 6, D_MODEL)
    cos_t, sin_t = _rope_tables(n)
    qa, ka, va, qb, kb, vb, ga, gb = _inproj(x, mod3, p['g_norm1'], p['g_qnorm'], p['g_knorm'],
                                             cos_t, sin_t, p['w_in'])
    oa = _gqa(qa, ka, va)
    ob = _diff(qb, kb, vb, p['bias_t'], p['cb'], p['lam4'], p['g_subln'])
    x1, u, g = _mid(x, oa, ob, ga, gb, mod3, p['g_norm2'], p['w_out'], p['w_ffn_in'])
    return _ffn_out(x1, u, g, p['conv_w'], p['conv_b'], mod3, p['g_final'], p['w_down'])


def kernel(x_prompt, x_sample, c_prompt, c_sample, w_mod, b_mod, g_norm1, w_in, g_qnorm, g_knorm,
           lambda_q1, lambda_k1, lambda_q2, lambda_k2, g_subln, rel_bias, w_out, g_norm2,
           w_ffn_in, conv_w, conv_b, w_down, g_final):
    bp, bs = x_prompt.shape[0], x_sample.shape[0]
    rows = bp + bs
    pad = (-rows) % 8
    c_all = jnp.pad(jnp.concatenate([c_prompt, c_sample], axis=0), ((0, pad), (0, 0)))
    mod = _mod(c_all, w_mod[0], b_mod)

    p = {
        'g_norm1': g_norm1, 'g_qnorm': g_qnorm, 'g_knorm': g_knorm, 'g_subln': g_subln,
        'g_norm2': g_norm2, 'g_final': g_final.reshape(1, D_MODEL),
        'w_in': w_in[0].astype(BF16), 'w_out': w_out[0].astype(BF16),
        'w_ffn_in': w_ffn_in[0].astype(BF16), 'w_down': w_down[0].astype(BF16),
        'conv_w': conv_w[0], 'conv_b': conv_b,
        'lam4': jnp.concatenate([lambda_q1, lambda_k1, lambda_q2, lambda_k2], axis=0),
        'bias_t': _bias_tiles(rel_bias, T_B),
        'cb': jnp.stack([rel_bias[N_BUCKETS // 2 - 1], rel_bias[N_BUCKETS - 1]], axis=1) * LOG2E,
    }
    y_prompt = _trunk(x_prompt, mod[:bp], p)
    y_sample = _trunk(x_sample, mod[bp:rows], p)
    return (y_prompt, y_sample)
```

```python
import functools
import math

import jax
import jax.numpy as jnp
from jax import lax
from jax.experimental import pallas as pl
from jax.experimental.pallas import tpu as pltpu

F32 = jnp.float32
BF16 = jnp.bfloat16

D_MODEL = 1024
GRID_W = 64
EPS = 1e-6
HA_Q, HA_KV, G_A, HD_A = 8, 2, 4, 128
ROPE_AXIS_DIM = HD_A // 2
ROPE_THETA = 10000.0
HB, DH_B, DV_B = 8, 64, 128
N_BUCKETS, MAX_DIST = 32, 128
D_FF = 2816
CONV_W = 3
LAMBDA_INIT = 0.8 - 0.6 * math.exp(-0.3 * 0)
LOG2E = 1.4426950408889634
SCALE_A = HD_A ** -0.5 * LOG2E
SCALE_B = DH_B ** -0.5 * LOG2E
NEG_BIG = -0.7 * float(jnp.finfo(jnp.float32).max)

OFF_QA, OFF_KA, OFF_VA = 0, 1024, 1280
OFF_QB, OFF_KB, OFF_VB = 1536, 2560, 3584
OFF_GA, OFF_GB = 4608, 5632
N_IN = 6656

LANES = 128
VMEM_LIMIT = 56 * 1024 * 1024

TM_IN = 512
TM_MID = 256
TM_FFN = 256
T_A = 512
T_B = 512


def _cparams(sem):
    return pltpu.CompilerParams(dimension_semantics=sem, vmem_limit_bytes=VMEM_LIMIT)


def _const_spec(shape):
    nd = len(shape)
    return pl.BlockSpec(shape, lambda *_: (0,) * nd, pipeline_mode=pl.Buffered(1))


def _mod_kernel(c_ref, w_ref, b_ref, o_ref):
    c = c_ref[...]
    sc = c * jax.nn.sigmoid(c)
    o_ref[...] = jnp.dot(sc, w_ref[...], preferred_element_type=F32,
                         precision=lax.Precision.HIGHEST) + b_ref[...]


def _mod(c, w_mod, b_mod):
    rows = c.shape[0]
    n_out = w_mod.shape[1]
    blk = 1024
    return pl.pallas_call(
        _mod_kernel,
        grid=(n_out // blk,),
        in_specs=[pl.BlockSpec((rows, D_MODEL), lambda j: (0, 0)),
                  pl.BlockSpec((D_MODEL, blk), lambda j: (0, j)),
                  pl.BlockSpec((1, blk), lambda j: (0, j))],
        out_specs=pl.BlockSpec((rows, blk), lambda j: (0, j)),
        out_shape=jax.ShapeDtypeStruct((rows, n_out), F32),
        compiler_params=_cparams(("parallel",)),
        name="mod",
    )(c, w_mod, b_mod)


def _bias_kernel(tab_ref, o_ref, *, t):
    cls = pl.program_id(0)
    row = lax.broadcasted_iota(jnp.int32, (t, t), 0)
    col = lax.broadcasted_iota(jnp.int32, (t, t), 1)
    rel = (cls - 1) * t + col - row
    nb = N_BUCKETS // 2
    max_exact = nb // 2
    ret = jnp.where(rel > 0, nb, 0)
    n = jnp.abs(rel)
    large = max_exact + (jnp.log(jnp.maximum(n, 1).astype(F32) / max_exact)
                         / math.log(MAX_DIST / max_exact) * (nb - max_exact)).astype(jnp.int32)
    large = jnp.minimum(large, nb - 1)
    bucket = ret + jnp.where(n < max_exact, n, large)
    for h in range(HB):
        acc = jnp.zeros((t, t), F32)
        for b in range(N_BUCKETS):
            acc = jnp.where(bucket == b, tab_ref[b, h], acc)
        o_ref[h, 0] = acc * LOG2E


def _bias_tiles(rel_bias, t):
    return pl.pallas_call(
        functools.partial(_bias_kernel, t=t),
        grid=(3,),
        in_specs=[pl.BlockSpec(memory_space=pltpu.SMEM)],
        out_specs=pl.BlockSpec((HB, 1, t, t), lambda c: (0, c, 0, 0)),
        out_shape=jax.ShapeDtypeStruct((HB, 3, t, t), F32),
        compiler_params=_cparams(("parallel",)),
        name="bias_tiles",
    )(rel_bias)


def _inproj_kernel(x_ref, mod_ref, g1_ref, gq_ref, gk_ref, cos_ref, sin_ref, w_ref,
                   qa_ref, ka_ref, va_ref, qb_ref, kb_ref, vb_ref, ga_ref, gb_ref, *, tm):
    x = x_ref[0]
    y = x * lax.rsqrt(jnp.mean(x * x, axis=-1, keepdims=True) + EPS)
    sh1 = mod_ref[0, 0:1, :]
    sc1 = mod_ref[0, 1:2, :]
    hb = ((y * g1_ref[...]) * (1.0 + sc1) + sh1).astype(BF16)

    cos = cos_ref[...]
    sin = sin_ref[...]
    lane = lax.broadcasted_iota(jnp.int32, (tm, LANES), 1)
    low_half = (lane & 32) == 0

    def proj(c0, width):
        return jnp.dot(hb, w_ref[:, c0:c0 + width], preferred_element_type=F32)

    def norm_rope(p, g, scale):
        yn = p * lax.rsqrt(jnp.mean(p * p, axis=-1, keepdims=True) + EPS) * g
        partner = jnp.where(low_half, pltpu.roll(yn, 96, 1), pltpu.roll(yn, 32, 1))
        r = yn * cos + partner * sin
        return r * scale if scale is not None else r

    gq = gq_ref[...]
    gk = gk_ref[...]
    for c in range(2):
        p = proj(OFF_QA + c * 512, 512)
        for j in range(4):
            qa_ref[0, c * 4 + j] = norm_rope(p[:, j * LANES:(j + 1) * LANES], gq, SCALE_A).astype(BF16)
    p = proj(OFF_KA, 512)
    for j in range(2):
        ka_ref[0, j] = norm_rope(p[:, j * LANES:(j + 1) * LANES], gk, None).astype(BF16)
        va_ref[0, j] = p[:, (2 + j) * LANES:(3 + j) * LANES].astype(BF16)
    for c in range(2):
        p = proj(OFF_QB + c * 512, 512)
        for j in range(4):
            qb_ref[0, c * 4 + j] = (p[:, j * LANES:(j + 1) * LANES] * SCALE_B).astype(BF16)
    for c in range(2):
        p = proj(OFF_KB + c * 512, 512)
        for j in range(4):
            kb_ref[0, c * 4 + j] = p[:, j * LANES:(j + 1) * LANES].astype(BF16)
    for c in range(2):
        p = proj(OFF_VB + c * 512, 512)
        for j in range(4):
            vb_ref[0, c * 4 + j] = p[:, j * LANES:(j + 1) * LANES].astype(BF16)
    for c in range(2):
        ga_ref[0, :, c * 512:(c + 1) * 512] = proj(OFF_GA + c * 512, 512)
    for c in range(2):
        gb_ref[0, :, c * 512:(c + 1) * 512] = proj(OFF_GB + c * 512, 512)


def _inproj(x, mod3, g1, gq, gk, cos_t, sin_t, w_in):
    b, n, _ = x.shape
    tm = TM_IN
    hm = lambda heads: pl.BlockSpec((1, heads, tm, LANES), lambda bi, i: (bi, 0, i, 0))
    tok = pl.BlockSpec((1, tm, D_MODEL), lambda bi, i: (bi, i, 0))
    hshape = lambda heads: jax.ShapeDtypeStruct((b, heads, n, LANES), BF16)
    return pl.pallas_call(
        functools.partial(_inproj_kernel, tm=tm),
        grid=(b, n // tm),
        in_specs=[tok,
                  pl.BlockSpec((1, 6, D_MODEL), lambda bi, i: (bi, 0, 0)),
                  _const_spec((1, D_MODEL)), _const_spec((1, LANES)), _const_spec((1, LANES)),
                  pl.BlockSpec((tm, LANES), lambda bi, i: (i, 0)),
                  pl.BlockSpec((tm, LANES), lambda bi, i: (i, 0)),
                  _const_spec((D_MODEL, N_IN))],
        out_specs=[hm(HA_Q), hm(HA_KV), hm(HA_KV), hm(HB), hm(HB), hm(HB), tok, tok],
        out_shape=[hshape(HA_Q), hshape(HA_KV), hshape(HA_KV), hshape(HB), hshape(HB), hshape(HB),
                   jax.ShapeDtypeStruct((b, n, D_MODEL), F32), jax.ShapeDtypeStruct((b, n, D_MODEL), F32)],
        compiler_params=_cparams(("parallel", "parallel")),
        name="inproj",
    )(x, mod3, g1, gq, gk, cos_t, sin_t, w_in)


def _stage_scores(s, s_slot, mc_slot):
    s_slot[...] = s
    mc_slot[...] = jnp.broadcast_to(jnp.max(s, axis=1, keepdims=True), mc_slot.shape)


def _stage_update(s_slot, mc_slot, shift, m_ref, accl_ref, v, tk):
    m_prev = m_ref[...]
    if shift is None:
        m_next = jnp.maximum(m_prev, mc_slot[...])
        sub = m_next
    else:
        m_next = jnp.maximum(m_prev, mc_slot[...] + shift)
        sub = m_next - shift
    p = jnp.exp2(s_slot[...] - pltpu.repeat(sub, tk // LANES, 1)).astype(BF16)
    alpha = jnp.exp2(m_prev - m_next)
    v_ones = jnp.concatenate([v, jnp.ones((tk, LANES), BF16)], axis=1)
    m_ref[...] = m_next
    accl_ref[...] = accl_ref[...] * pltpu.repeat(alpha, 2, 1) + jnp.dot(p, v_ones, preferred_element_type=F32)


def _pipelined(n_units, scores_fn, update_fn):
    scores_fn(0, 0)
    for u in range(n_units):
        if u + 1 < n_units:
            scores_fn(u + 1, (u + 1) % 2)
        update_fn(u, u % 2)


def _gqa_kernel(q_ref, k_ref, v_ref, o_ref, m_sc, accl_sc, s_sc, mc_sc, *, tq, ts, n_sub, nk):
    ki = pl.program_id(2)
    pairs = HA_Q // 2

    @pl.when(ki == 0)
    def _():
        m_sc[...] = jnp.full(m_sc.shape, NEG_BIG, F32)
        accl_sc[...] = jnp.zeros(accl_sc.shape, F32)

    def scores(u, slot):
        sub, pr = divmod(u, pairs)
        kv = pr // (G_A // 2)
        q = q_ref[0, 2 * pr:2 * pr + 2].reshape(2 * tq, HD_A)
        k = k_ref[0, kv, sub * ts:(sub + 1) * ts]
        s = lax.dot_general(q, k, (((1,), (1,)), ((), ())), preferred_element_type=F32)
        _stage_scores(s, s_sc.at[slot], mc_sc.at[slot])

    def update(u, slot):
        sub, pr = divmod(u, pairs)
        kv = pr // (G_A // 2)
        _stage_update(s_sc.at[slot], mc_sc.at[slot], None, m_sc.at[pr], accl_sc.at[pr],
                      v_ref[0, kv, sub * ts:(sub + 1) * ts], ts)

    _pipelined(n_sub * pairs, scores, update)

    @pl.when(ki == nk - 1)
    def _():
        for pr in range(pairs):
            o = accl_sc[pr, :, :HD_A] / accl_sc[pr, :, HD_A:]
            for g in range(2):
                h = 2 * pr + g
                o_ref[0, :, h * HD_A:(h + 1) * HD_A] = o[g * tq:(g + 1) * tq]


def _gqa(qa, ka, va):
    b, _, n, _ = qa.shape
    tq, ts, n_sub = T_A, T_A, 2
    tk = ts * n_sub
    nq, nk = n // tq, n // tk
    rows = 2 * tq
    return pl.pallas_call(
        functools.partial(_gqa_kernel, tq=tq, ts=ts, n_sub=n_sub, nk=nk),
        grid=(b, nq, nk),
        in_specs=[pl.BlockSpec((1, HA_Q, tq, HD_A), lambda bi, qi, ki: (bi, 0, qi, 0)),
                  pl.BlockSpec((1, HA_KV, tk, HD_A), lambda bi, qi, ki: (bi, 0, ki, 0)),
                  pl.BlockSpec((1, HA_KV, tk, HD_A), lambda bi, qi, ki: (bi, 0, ki, 0))],
        out_specs=pl.BlockSpec((1, tq, D_MODEL), lambda bi, qi, ki: (bi, qi, 0)),
        out_shape=jax.ShapeDtypeStruct((b, n, D_MODEL), F32),
        scratch_shapes=[pltpu.VMEM((HA_Q // 2, rows, LANES), F32),
                        pltpu.VMEM((HA_Q // 2, rows, HD_A + LANES), F32),
                        pltpu.VMEM((2, rows, ts), F32),
                        pltpu.VMEM((2, rows, LANES), F32)],
        compiler_params=_cparams(("parallel", "parallel", "arbitrary")),
        name="gqa",
    )(qa, ka, va)


def _diff_kernel(cb_ref, lam_ref, gs_ref, q_ref, k_ref, v_ref, bias_ref, o_ref,
                 q2_sc, m_sc, accl_sc, s_sc, mc_sc, *, t, nk):
    qi = pl.program_id(1)
    ki = pl.program_id(2)

    @pl.when(ki == 0)
    def _():
        m_sc[...] = jnp.full(m_sc.shape, NEG_BIG, F32)
        accl_sc[...] = jnp.zeros(accl_sc.shape, F32)
        first = lax.broadcasted_iota(jnp.int32, (t, LANES), 1) < DH_B
        zero = jnp.zeros((t, LANES), BF16)
        for h in range(HB):
            q = q_ref[0, h]
            q2_sc[h, :t] = jnp.where(first, q, zero)
            q2_sc[h, t:] = jnp.where(first, zero, q)

    def heads(near):
        def scores(h, slot):
            s = lax.dot_general(q2_sc[h], k_ref[0, h], (((1,), (1,)), ((), ())), preferred_element_type=F32)
            if near:
                s = (s.reshape(2, t, t) + bias_ref[h, 0][None]).reshape(2 * t, t)
            _stage_scores(s, s_sc.at[slot], mc_sc.at[slot])

        def update(h, slot):
            shift = None if near else jnp.where(ki > qi, cb_ref[h, 1], cb_ref[h, 0])
            _stage_update(s_sc.at[slot], mc_sc.at[slot], shift, m_sc.at[h], accl_sc.at[h], v_ref[0, h], t)

        _pipelined(HB, scores, update)

    near_diag = jnp.abs(ki - qi) <= 1

    @pl.when(near_diag)
    def _():
        heads(True)

    @pl.when(jnp.logical_not(near_diag))
    def _():
        heads(False)

    @pl.when(ki == nk - 1)
    def _():
        lam = (jnp.exp(jnp.sum(lam_ref[0:1, :] * lam_ref[1:2, :], axis=-1, keepdims=True))
               - jnp.exp(jnp.sum(lam_ref[2:3, :] * lam_ref[3:4, :], axis=-1, keepdims=True))
               + LAMBDA_INIT)
        gs = gs_ref[...]
        for h in range(HB):
            o12 = accl_sc[h, :, :DV_B] / accl_sc[h, :, DV_B:]
            o = o12[:t] - lam * o12[t:]
            o = o * lax.rsqrt(jnp.mean(o * o, axis=-1, keepdims=True) + EPS) * gs
            o_ref[0, :, h * DV_B:(h + 1) * DV_B] = o * (1.0 - LAMBDA_INIT)


def _diff(qb, kb, vb, bias_t, cb, lam4, gs):
    b, _, n, _ = qb.shape
    t = T_B
    nq = nk = n // t
    hm = lambda idx: pl.BlockSpec((1, HB, t, LANES), idx)
    return pl.pallas_call(
        functools.partial(_diff_kernel, t=t, nk=nk),
        grid=(b, nq, nk),
        in_specs=[pl.BlockSpec(memory_space=pltpu.SMEM),
                  _const_spec((4, DH_B)), _const_spec((1, DV_B)),
                  hm(lambda bi, qi, ki: (bi, 0, qi, 0)),
                  hm(lambda bi, qi, ki: (bi, 0, ki, 0)),
                  hm(lambda bi, qi, ki: (bi, 0, ki, 0)),
                  pl.BlockSpec((HB, 1, t, t), lambda bi, qi, ki: (0, jnp.clip(ki - qi, -1, 1) + 1, 0, 0))],
        out_specs=pl.BlockSpec((1, t, D_MODEL), lambda bi, qi, ki: (bi, qi, 0)),
        out_shape=jax.ShapeDtypeStruct((b, n, D_MODEL), F32),
        scratch_shapes=[pltpu.VMEM((HB, 2 * t, LANES), BF16),
                        pltpu.VMEM((HB, 2 * t, LANES), F32),
                        pltpu.VMEM((HB, 2 * t, DV_B + LANES), F32),
                        pltpu.VMEM((2, 2 * t, t), F32),
                        pltpu.VMEM((2, 2 * t, LANES), F32)],
        compiler_params=_cparams(("parallel", "parallel", "arbitrary")),
        name="diff",
    )(cb, lam4, gs, qb, kb, vb, bias_t)


def _mid_kernel(x_ref, oa_ref, ob_ref, ga_ref, gb_ref, mod_ref, g2_ref, wo_ref, wf_ref,
                x1_ref, u_ref, g_ref):
    merged = (jax.nn.sigmoid(ga_ref[0]) * oa_ref[0] + jax.nn.sigmoid(gb_ref[0]) * ob_ref[0]).astype(BF16)
    gt1 = mod_ref[0, 2:3, :]
    sh2 = mod_ref[0, 3:4, :]
    sc2 = mod_ref[0, 4:5, :]
    x1 = x_ref[0] + gt1 * jnp.dot(merged, wo_ref[...], preferred_element_type=F32)
    x1_ref[0] = x1
    y = x1 * lax.rsqrt(jnp.mean(x1 * x1, axis=-1, keepdims=True) + EPS)
    h2 = ((y * g2_ref[...]) * (1.0 + sc2) + sh2).astype(BF16)
    cw = 256
    for c in range(D_FF // cw):
        u_ref[0, :, c * cw:(c + 1) * cw] = jnp.dot(
            h2, wf_ref[:, c * cw:(c + 1) * cw], preferred_element_type=F32).astype(BF16)
        g_ref[0, :, c * cw:(c + 1) * cw] = jnp.dot(
            h2, wf_ref[:, D_FF + c * cw:D_FF + (c + 1) * cw], preferred_element_type=F32)


def _mid(x, oa, ob, ga, gb, mod3, g2, w_out, w_ffn_in):
    b, n, _ = x.shape
    tm = TM_MID
    tok = pl.BlockSpec((1, tm, D_MODEL), lambda bi, i: (bi, i, 0))
    ff = pl.BlockSpec((1, tm, D_FF), lambda bi, i: (bi, i, 0))
    return pl.pallas_call(
        _mid_kernel,
        grid=(b, n // tm),
        in_specs=[tok, tok, tok, tok, tok,
                  pl.BlockSpec((1, 6, D_MODEL), lambda bi, i: (bi, 0, 0)),
                  _const_spec((1, D_MODEL)),
                  _const_spec((D_MODEL, D_MODEL)),
                  _const_spec((D_MODEL, 2 * D_FF))],
        out_specs=[tok, ff, ff],
        out_shape=[jax.ShapeDtypeStruct((b, n, D_MODEL), F32),
                   jax.ShapeDtypeStruct((b, n, D_FF), BF16),
                   jax.ShapeDtypeStruct((b, n, D_FF), F32)],
        compiler_params=_cparams(("parallel", "parallel")),
        name="mid",
    )(x, oa, ob, ga, gb, mod3, g2, w_out, w_ffn_in)


def _ffn_out_kernel(x1_ref, u_ref, g_ref, gp_ref, gn_ref, cw_ref, cbias_ref, mod_ref, gf_ref, wd_ref,
                    y_ref, gs_sc, *, tm, nt):
    i = pl.program_id(1)
    g = g_ref[0]
    gs_sc[8:8 + tm, :] = g
    gs_sc[7:8, :] = jnp.where(i > 0, gp_ref[0, 7:8, :], 0.0)
    gs_sc[8 + tm:9 + tm, :] = jnp.where(i < nt - 1, gn_ref[0, 0:1, :], 0.0)
    z = (cbias_ref[...] + gs_sc[7:7 + tm, :] * cw_ref[0:1, :] + g * cw_ref[1:2, :]
         + gs_sc[9:9 + tm, :] * cw_ref[2:3, :])
    gelu = 0.5 * z * (1.0 + lax.erf(z * math.sqrt(0.5)))
    a = (gelu * u_ref[0].astype(F32)).astype(BF16)
    gt2 = mod_ref[0, 5:6, :]
    x2 = x1_ref[0] + gt2 * jnp.dot(a, wd_ref[...], preferred_element_type=F32)
    y_ref[0] = x2 * lax.rsqrt(jnp.mean(x2 * x2, axis=-1, keepdims=True) + EPS) * gf_ref[...]


def _ffn_out(x1, u, g, conv_w, conv_b, mod3, g_final, w_down):
    b, n, _ = x1.shape
    tm = TM_FFN
    nt = n // tm
    r8 = tm // 8
    tok = pl.BlockSpec((1, tm, D_MODEL), lambda bi, i: (bi, i, 0))
    ff = pl.BlockSpec((1, tm, D_FF), lambda bi, i: (bi, i, 0))
    return pl.pallas_call(
        functools.partial(_ffn_out_kernel, tm=tm, nt=nt),
        grid=(b, nt),
        in_specs=[tok, ff, ff,
                  pl.BlockSpec((1, 8, D_FF), lambda bi, i: (bi, jnp.maximum(i * r8 - 1, 0), 0)),
                  pl.BlockSpec((1, 8, D_FF), lambda bi, i: (bi, jnp.minimum((i + 1) * r8, n // 8 - 1), 0)),
                  _const_spec((CONV_W, D_FF)), _const_spec((1, D_FF)),
                  pl.BlockSpec((1, 6, D_MODEL), lambda bi, i: (bi, 0, 0)),
                  _const_spec((1, D_MODEL)),
                  _const_spec((D_FF, D_MODEL))],
        out_specs=tok,
        out_shape=jax.ShapeDtypeStruct((b, n, D_MODEL), F32),
        scratch_shapes=[pltpu.VMEM((tm + 16, D_FF), F32)],
        compiler_params=_cparams(("parallel", "parallel")),
        name="ffn_out",
    )(x1, u, g, g, g, conv_w, conv_b, mod3, g_final, w_down)


def _rope_tables(n):
    t = jnp.arange(n, dtype=jnp.int32)
    row = (t // GRID_W).astype(F32)
    col = (t % GRID_W).astype(F32)
    inv = ROPE_THETA ** (-jnp.arange(0, ROPE_AXIS_DIM, 2, dtype=F32) / ROPE_AXIS_DIM)
    ar = row[:, None] * inv[None, :]
    ac = col[:, None] * inv[None, :]
    cos_t = jnp.concatenate([jnp.cos(ar), jnp.cos(ar), jnp.cos(ac), jnp.cos(ac)], axis=-1)
    sin_t = jnp.concatenate([-jnp.sin(ar), jnp.sin(ar), -jnp.sin(ac), jnp.sin(ac)], axis=-1)
    return cos_t, sin_t


def _trunk(x, mod, p):
    b, n, _ = x.shape
    mod3 = mod.reshape(b, 6, D_MODEL)
    cos_t, sin_t = _rope_tables(n)
    qa, ka, va, qb, kb, vb, ga, gb = _inproj(x, mod3, p['g_norm1'], p['g_qnorm'], p['g_knorm'],
                                             cos_t, sin_t, p['w_in'])
    oa = _gqa(qa, ka, va)
    ob = _diff(qb, kb, vb, p['bias_t'], p['cb'], p['lam4'], p['g_subln'])
    x1, u, g = _mid(x, oa, ob, ga, gb, mod3, p['g_norm2'], p['w_out'], p['w_ffn_in'])
    return _ffn_out(x1, u, g, p['conv_w'], p['conv_b'], mod3, p['g_final'], p['w_down'])


def kernel(x_prompt, x_sample, c_prompt, c_sample, w_mod, b_mod, g_norm1, w_in, g_qnorm, g_knorm,
           lambda_q1, lambda_k1, lambda_q2, lambda_k2, g_subln, rel_bias, w_out, g_norm2,
           w_ffn_in, conv_w, conv_b, w_down, g_final):
    bp, bs = x_prompt.shape[0], x_sample.shape[0]
    rows = bp + bs
    pad = (-rows) % 8
    c_all = jnp.pad(jnp.concatenate([c_prompt, c_sample], axis=0), ((0, pad), (0, 0)))
    mod = _mod(c_all, w_mod[0], b_mod)

    p = {
        'g_norm1': g_norm1, 'g_qnorm': g_qnorm, 'g_knorm': g_knorm, 'g_subln': g_subln,
        'g_norm2': g_norm2, 'g_final': g_final.reshape(1, D_MODEL),
        'w_in': w_in[0].astype(BF16), 'w_out': w_out[0].astype(BF16),
        'w_ffn_in': w_ffn_in[0].astype(BF16), 'w_down': w_down[0].astype(BF16),
        'conv_w': conv_w[0], 'conv_b': conv_b,
        'lam4': jnp.concatenate([lambda_q1, lambda_k1, lambda_q2, lambda_k2], axis=0),
        'bias_t': _bias_tiles(rel_bias, T_B),
        'cb': jnp.stack([rel_bias[N_BUCKETS // 2 - 1], rel_bias[N_BUCKETS - 1]], axis=1) * LOG2E,
    }
    y_prompt = _trunk(x_prompt, mod[:bp], p)
    y_sample = _trunk(x_sample, mod[bp:rows], p)
    return (y_prompt, y_sample)
```

```python
import functools
import math

import jax
import jax.numpy as jnp
from jax import lax
from jax.experimental import pallas as pl
from jax.experimental.pallas import tpu as pltpu

F32 = jnp.float32
BF16 = jnp.bfloat16

D_MODEL = 1024
GRID_W = 64
EPS = 1e-6
HA_Q, HA_KV, G_A, HD_A = 8, 2, 4, 128
ROPE_AXIS_DIM = HD_A // 2
ROPE_THETA = 10000.0
HB, DH_B, DV_B = 8, 64, 128
N_BUCKETS, MAX_DIST = 32, 128
D_FF = 2816
CONV_W = 3
LAMBDA_INIT = 0.8 - 0.6 * math.exp(-0.3 * 0)
LOG2E = 1.4426950408889634
SCALE_A = HD_A ** -0.5 * LOG2E
SCALE_B = DH_B ** -0.5 * LOG2E
NEG_BIG = -0.7 * float(jnp.finfo(jnp.float32).max)

OFF_QA, OFF_KA, OFF_VA = 0, 1024, 1280
OFF_QB, OFF_KB, OFF_VB = 1536, 2560, 3584
OFF_GA, OFF_GB = 4608, 5632
N_IN = 6656

LANES = 128
VMEM_LIMIT = 56 * 1024 * 1024

TM_IN = 512
TM_MID = 512
TM_FFN = 256
HALO = 16
FF_CHUNK = 256
T_A = 512
T_B = 512


def _cparams(sem):
    return pltpu.CompilerParams(dimension_semantics=sem, vmem_limit_bytes=VMEM_LIMIT)


def _const_spec(shape):
    nd = len(shape)
    return pl.BlockSpec(shape, lambda *_: (0,) * nd, pipeline_mode=pl.Buffered(1))


def _mod_kernel(c_ref, w_ref, b_ref, o_ref):
    c = c_ref[...]
    sc = c * jax.nn.sigmoid(c)
    o_ref[...] = jnp.dot(sc, w_ref[...], preferred_element_type=F32,
                         precision=lax.Precision.HIGHEST) + b_ref[...]


def _mod(c, w_mod, b_mod):
    rows = c.shape[0]
    n_out = w_mod.shape[1]
    blk = 1024
    return pl.pallas_call(
        _mod_kernel,
        grid=(n_out // blk,),
        in_specs=[pl.BlockSpec((rows, D_MODEL), lambda j: (0, 0)),
                  pl.BlockSpec((D_MODEL, blk), lambda j: (0, j)),
                  pl.BlockSpec((1, blk), lambda j: (0, j))],
        out_specs=pl.BlockSpec((rows, blk), lambda j: (0, j)),
        out_shape=jax.ShapeDtypeStruct((rows, n_out), F32),
        compiler_params=_cparams(("parallel",)),
        name="mod",
    )(c, w_mod, b_mod)


def _bias_kernel(tab_ref, o_ref, *, t):
    cls = pl.program_id(0)
    row = lax.broadcasted_iota(jnp.int32, (t, t), 0)
    col = lax.broadcasted_iota(jnp.int32, (t, t), 1)
    rel = (cls - 1) * t + col - row
    nb = N_BUCKETS // 2
    max_exact = nb // 2
    ret = jnp.where(rel > 0, nb, 0)
    n = jnp.abs(rel)
    large = max_exact + (jnp.log(jnp.maximum(n, 1).astype(F32) / max_exact)
                         / math.log(MAX_DIST / max_exact) * (nb - max_exact)).astype(jnp.int32)
    large = jnp.minimum(large, nb - 1)
    bucket = ret + jnp.where(n < max_exact, n, large)
    for h in range(HB):
        acc = jnp.zeros((t, t), F32)
        for b in range(N_BUCKETS):
            acc = jnp.where(bucket == b, tab_ref[b, h], acc)
        o_ref[h, 0] = acc * LOG2E


def _bias_tiles(rel_bias, t):
    return pl.pallas_call(
        functools.partial(_bias_kernel, t=t),
        grid=(3,),
        in_specs=[pl.BlockSpec(memory_space=pltpu.SMEM)],
        out_specs=pl.BlockSpec((HB, 1, t, t), lambda c: (0, c, 0, 0)),
        out_shape=jax.ShapeDtypeStruct((HB, 3, t, t), F32),
        compiler_params=_cparams(("parallel",)),
        name="bias_tiles",
    )(rel_bias)


def _inproj_kernel(x_ref, mod_ref, g1_ref, gq_ref, gk_ref, cos_ref, sin_ref, w_ref,
                   qa_ref, ka_ref, va_ref, qb_ref, kb_ref, vb_ref, ga_ref, gb_ref, *, tm):
    x = x_ref[0]
    y = x * lax.rsqrt(jnp.mean(x * x, axis=-1, keepdims=True) + EPS)
    sh1 = mod_ref[0, 0:1, :]
    sc1 = mod_ref[0, 1:2, :]
    hb = ((y * g1_ref[...]) * (1.0 + sc1) + sh1).astype(BF16)

    cos = cos_ref[...]
    sin = sin_ref[...]
    lane = lax.broadcasted_iota(jnp.int32, (tm, LANES), 1)
    low_half = (lane & 32) == 0

    def proj(c0, width):
        return jnp.dot(hb, w_ref[:, c0:c0 + width], preferred_element_type=F32)

    def norm_rope(p, g, scale):
        yn = p * lax.rsqrt(jnp.mean(p * p, axis=-1, keepdims=True) + EPS) * g
        partner = jnp.where(low_half, pltpu.roll(yn, 96, 1), pltpu.roll(yn, 32, 1))
        r = yn * cos + partner * sin
        return r * scale if scale is not None else r

    gq = gq_ref[...]
    gk = gk_ref[...]
    for c in range(2):
        p = proj(OFF_QA + c * 512, 512)
        for j in range(4):
            qa_ref[0, c * 4 + j] = norm_rope(p[:, j * LANES:(j + 1) * LANES], gq, SCALE_A).astype(BF16)
    p = proj(OFF_KA, 512)
    for j in range(2):
        ka_ref[0, j] = norm_rope(p[:, j * LANES:(j + 1) * LANES], gk, None).astype(BF16)
        va_ref[0, j] = p[:, (2 + j) * LANES:(3 + j) * LANES].astype(BF16)
    for c in range(2):
        p = proj(OFF_QB + c * 512, 512)
        for j in range(4):
            qb_ref[0, c * 4 + j] = (p[:, j * LANES:(j + 1) * LANES] * SCALE_B).astype(BF16)
    for c in range(2):
        p = proj(OFF_KB + c * 512, 512)
        for j in range(4):
            kb_ref[0, c * 4 + j] = p[:, j * LANES:(j + 1) * LANES].astype(BF16)
    for c in range(2):
        p = proj(OFF_VB + c * 512, 512)
        for j in range(4):
            vb_ref[0, c * 4 + j] = p[:, j * LANES:(j + 1) * LANES].astype(BF16)
    for c in range(2):
        ga_ref[0, :, c * 512:(c + 1) * 512] = jax.nn.sigmoid(proj(OFF_GA + c * 512, 512)).astype(BF16)
    for c in range(2):
        gb_ref[0, :, c * 512:(c + 1) * 512] = jax.nn.sigmoid(proj(OFF_GB + c * 512, 512)).astype(BF16)


def _inproj(x, mod3, g1, gq, gk, cos_t, sin_t, w_in):
    b, n, _ = x.shape
    tm = TM_IN
    hm = lambda heads: pl.BlockSpec((1, heads, tm, LANES), lambda bi, i: (bi, 0, i, 0))
    tok = pl.BlockSpec((1, tm, D_MODEL), lambda bi, i: (bi, i, 0))
    hshape = lambda heads: jax.ShapeDtypeStruct((b, heads, n, LANES), BF16)
    return pl.pallas_call(
        functools.partial(_inproj_kernel, tm=tm),
        grid=(b, n // tm),
        in_specs=[tok,
                  pl.BlockSpec((1, 6, D_MODEL), lambda bi, i: (bi, 0, 0)),
                  _const_spec((1, D_MODEL)), _const_spec((1, LANES)), _const_spec((1, LANES)),
                  pl.BlockSpec((tm, LANES), lambda bi, i: (i, 0)),
                  pl.BlockSpec((tm, LANES), lambda bi, i: (i, 0)),
                  _const_spec((D_MODEL, N_IN))],
        out_specs=[hm(HA_Q), hm(HA_KV), hm(HA_KV), hm(HB), hm(HB), hm(HB), tok, tok],
        out_shape=[hshape(HA_Q), hshape(HA_KV), hshape(HA_KV), hshape(HB), hshape(HB), hshape(HB),
                   jax.ShapeDtypeStruct((b, n, D_MODEL), BF16), jax.ShapeDtypeStruct((b, n, D_MODEL), BF16)],
        compiler_params=_cparams(("parallel", "parallel")),
        name="inproj",
    )(x, mod3, g1, gq, gk, cos_t, sin_t, w_in)


def _stage_scores(s, s_slot, mc_slot):
    s_slot[...] = s
    mc_slot[...] = jnp.broadcast_to(jnp.max(s, axis=1, keepdims=True), mc_slot.shape)


def _stage_update(s_slot, mc_slot, shift, m_ref, accl_ref, v, tk):
    m_prev = m_ref[...]
    if shift is None:
        m_next = jnp.maximum(m_prev, mc_slot[...])
        sub = m_next
    else:
        m_next = jnp.maximum(m_prev, mc_slot[...] + shift)
        sub = m_next - shift
    p = jnp.exp2(s_slot[...] - jnp.concatenate([sub] * (tk // LANES), axis=1)).astype(BF16)
    alpha = jnp.exp2(m_prev - m_next)
    v_ones = jnp.concatenate([v, jnp.ones((tk, LANES), BF16)], axis=1)
    m_ref[...] = m_next
    accl_ref[...] = (accl_ref[...] * jnp.concatenate([alpha, alpha], axis=1)
                     + jnp.dot(p, v_ones, preferred_element_type=F32))


def _pipelined(n_units, scores_fn, update_fn):
    scores_fn(0, 0)
    for u in range(n_units):
        if u + 1 < n_units:
            scores_fn(u + 1, (u + 1) % 2)
        update_fn(u, u % 2)


def _gqa_kernel(q_ref, k_ref, v_ref, o_ref, m_sc, accl_sc, s_sc, mc_sc, *, tq, ts, n_sub, nk):
    ki = pl.program_id(2)

    @pl.when(ki == 0)
    def _():
        m_sc[...] = jnp.full(m_sc.shape, NEG_BIG, F32)
        accl_sc[...] = jnp.zeros(accl_sc.shape, F32)

    def scores(u, slot):
        sub, h = divmod(u, HA_Q)
        k = k_ref[0, h // G_A, sub * ts:(sub + 1) * ts]
        s = lax.dot_general(q_ref[0, h], k, (((1,), (1,)), ((), ())), preferred_element_type=F32)
        _stage_scores(s, s_sc.at[slot], mc_sc.at[slot])

    def update(u, slot):
        sub, h = divmod(u, HA_Q)
        _stage_update(s_sc.at[slot], mc_sc.at[slot], None, m_sc.at[h], accl_sc.at[h],
                      v_ref[0, h // G_A, sub * ts:(sub + 1) * ts], ts)

    _pipelined(n_sub * HA_Q, scores, update)

    @pl.when(ki == nk - 1)
    def _():
        for h in range(HA_Q):
            o = accl_sc[h, :, :HD_A] / accl_sc[h, :, HD_A:]
            o_ref[0, :, h * HD_A:(h + 1) * HD_A] = o.astype(BF16)


def _gqa(qa, ka, va):
    b, _, n, _ = qa.shape
    tq, ts, n_sub = T_A, T_A, 2
    tk = ts * n_sub
    nq, nk = n // tq, n // tk
    return pl.pallas_call(
        functools.partial(_gqa_kernel, tq=tq, ts=ts, n_sub=n_sub, nk=nk),
        grid=(b, nq, nk),
        in_specs=[pl.BlockSpec((1, HA_Q, tq, HD_A), lambda bi, qi, ki: (bi, 0, qi, 0)),
                  pl.BlockSpec((1, HA_KV, tk, HD_A), lambda bi, qi, ki: (bi, 0, ki, 0)),
                  pl.BlockSpec((1, HA_KV, tk, HD_A), lambda bi, qi, ki: (bi, 0, ki, 0))],
        out_specs=pl.BlockSpec((1, tq, D_MODEL), lambda bi, qi, ki: (bi, qi, 0)),
        out_shape=jax.ShapeDtypeStruct((b, n, D_MODEL), BF16),
        scratch_shapes=[pltpu.VMEM((HA_Q, tq, LANES), F32),
                        pltpu.VMEM((HA_Q, tq, HD_A + LANES), F32),
                        pltpu.VMEM((2, tq, ts), F32),
                        pltpu.VMEM((2, tq, LANES), F32)],
        compiler_params=_cparams(("parallel", "parallel", "arbitrary")),
        name="gqa",
    )(qa, ka, va)


def _diff_kernel(cb_ref, lam_ref, gs_ref, q_ref, k_ref, v_ref, bias_ref, o_ref,
                 q2_sc, m_sc, accl_sc, s_sc, mc_sc, *, t, nk):
    qi = pl.program_id(1)
    ki = pl.program_id(2)

    @pl.when(ki == 0)
    def _():
        m_sc[...] = jnp.full(m_sc.shape, NEG_BIG, F32)
        accl_sc[...] = jnp.zeros(accl_sc.shape, F32)
        first = lax.broadcasted_iota(jnp.int32, (t, LANES), 1) < DH_B
        zero = jnp.zeros((t, LANES), BF16)
        for h in range(HB):
            q = q_ref[0, h]
            q2_sc[h, :t] = jnp.where(first, q, zero)
            q2_sc[h, t:] = jnp.where(first, zero, q)

    def heads(near):
        def scores(u, slot):
            h, half = divmod(u, 2)
            s = lax.dot_general(q2_sc[h, half * t:(half + 1) * t], k_ref[0, h], (((1,), (1,)), ((), ())),
                                preferred_element_type=F32)
            if near:
                s = s + bias_ref[h, 0]
            _stage_scores(s, s_sc.at[slot], mc_sc.at[slot])

        def update(u, slot):
            h, half = divmod(u, 2)
            rs = slice(half * t, (half + 1) * t)
            shift = None if near else jnp.where(ki > qi, cb_ref[h, 1], cb_ref[h, 0])
            _stage_update(s_sc.at[slot], mc_sc.at[slot], shift, m_sc.at[h, rs], accl_sc.at[h, rs], v_ref[0, h], t)

        _pipelined(2 * HB, scores, update)

    near_diag = jnp.abs(ki - qi) <= 1

    @pl.when(near_diag)
    def _():
        heads(True)

    @pl.when(jnp.logical_not(near_diag))
    def _():
        heads(False)

    @pl.when(ki == nk - 1)
    def _():
        lam = (jnp.exp(jnp.sum(lam_ref[0:1, :] * lam_ref[1:2, :], axis=-1, keepdims=True))
               - jnp.exp(jnp.sum(lam_ref[2:3, :] * lam_ref[3:4, :], axis=-1, keepdims=True))
               + LAMBDA_INIT)
        gs = gs_ref[...]
        for h in range(HB):
            o12 = accl_sc[h, :, :DV_B] / accl_sc[h, :, DV_B:]
            o = o12[:t] - lam * o12[t:]
            o = o * lax.rsqrt(jnp.mean(o * o, axis=-1, keepdims=True) + EPS) * gs
            o_ref[0, :, h * DV_B:(h + 1) * DV_B] = (o * (1.0 - LAMBDA_INIT)).astype(BF16)


def _diff(qb, kb, vb, bias_t, cb, lam4, gs):
    b, _, n, _ = qb.shape
    t = T_B
    nq = nk = n // t
    hm = lambda idx: pl.BlockSpec((1, HB, t, LANES), idx)
    return pl.pallas_call(
        functools.partial(_diff_kernel, t=t, nk=nk),
        grid=(b, nq, nk),
        in_specs=[pl.BlockSpec(memory_space=pltpu.SMEM),
                  _const_spec((4, DH_B)), _const_spec((1, DV_B)),
                  hm(lambda bi, qi, ki: (bi, 0, qi, 0)),
                  hm(lambda bi, qi, ki: (bi, 0, ki, 0)),
                  hm(lambda bi, qi, ki: (bi, 0, ki, 0)),
                  pl.BlockSpec((HB, 1, t, t), lambda bi, qi, ki: (0, jnp.clip(ki - qi, -1, 1) + 1, 0, 0))],
        out_specs=pl.BlockSpec((1, t, D_MODEL), lambda bi, qi, ki: (bi, qi, 0)),
        out_shape=jax.ShapeDtypeStruct((b, n, D_MODEL), BF16),
        scratch_shapes=[pltpu.VMEM((HB, 2 * t, LANES), BF16),
                        pltpu.VMEM((HB, 2 * t, LANES), F32),
                        pltpu.VMEM((HB, 2 * t, DV_B + LANES), F32),
                        pltpu.VMEM((2, t, t), F32),
                        pltpu.VMEM((2, t, LANES), F32)],
        compiler_params=_cparams(("parallel", "parallel", "arbitrary")),
        name="diff",
    )(cb, lam4, gs, qb, kb, vb, bias_t)


def _mid_kernel(x_ref, oa_ref, ob_ref, sa_ref, sb_ref, mod_ref, g2_ref, wo_ref, x1_ref, h2_ref):
    merged = (sa_ref[0].astype(F32) * oa_ref[0].astype(F32)
              + sb_ref[0].astype(F32) * ob_ref[0].astype(F32)).astype(BF16)
    gt1 = mod_ref[0, 2:3, :]
    sh2 = mod_ref[0, 3:4, :]
    sc2 = mod_ref[0, 4:5, :]
    x1 = x_ref[0] + gt1 * jnp.dot(merged, wo_ref[...], preferred_element_type=F32)
    x1_ref[0] = x1
    y = x1 * lax.rsqrt(jnp.mean(x1 * x1, axis=-1, keepdims=True) + EPS)
    h2_ref[0] = ((y * g2_ref[...]) * (1.0 + sc2) + sh2).astype(BF16)


def _mid(x, oa, ob, sa, sb, mod3, g2, w_out):
    b, n, _ = x.shape
    tm = TM_MID
    tok = pl.BlockSpec((1, tm, D_MODEL), lambda bi, i: (bi, i, 0))
    return pl.pallas_call(
        _mid_kernel,
        grid=(b, n // tm),
        in_specs=[tok, tok, tok, tok, tok,
                  pl.BlockSpec((1, 6, D_MODEL), lambda bi, i: (bi, 0, 0)),
                  _const_spec((1, D_MODEL)),
                  _const_spec((D_MODEL, D_MODEL))],
        out_specs=[tok, tok],
        out_shape=[jax.ShapeDtypeStruct((b, n, D_MODEL), F32),
                   jax.ShapeDtypeStruct((b, n, D_MODEL), BF16)],
        compiler_params=_cparams(("parallel", "parallel")),
        name="mid",
    )(x, oa, ob, sa, sb, mod3, g2, w_out)


def _ffn_kernel(x1_ref, h_ref, hp_ref, hn_ref, mod_ref, wf_ref, cw_ref, cbias_ref, gf_ref, wd_ref,
                y_ref, gs_sc, us_sc, *, tm, nt):
    i = pl.program_id(1)
    rows = tm + 2 * HALO
    h2 = jnp.concatenate([hp_ref[0], h_ref[0], hn_ref[0]], axis=0)
    h2m = h_ref[0]
    ridx = lax.broadcasted_iota(jnp.int32, (rows, 1), 0)
    keep = jnp.logical_and(jnp.logical_or(i > 0, ridx >= HALO), jnp.logical_or(i < nt - 1, ridx < HALO + tm))
    n_chunks = D_FF // FF_CHUNK

    def ffn_in(c, slot):
        cs = slice(c * FF_CHUNK, (c + 1) * FF_CHUNK)
        gcs = slice(D_FF + c * FF_CHUNK, D_FF + (c + 1) * FF_CHUNK)
        us_sc[slot] = jnp.dot(h2m, wf_ref[:, cs], preferred_element_type=F32)
        gs_sc[slot] = jnp.where(keep, jnp.dot(h2, wf_ref[:, gcs], preferred_element_type=F32), 0.0)

    def ffn_out(c, slot, acc):
        cs = slice(c * FF_CHUNK, (c + 1) * FF_CHUNK)
        z = (cbias_ref[:, cs] + gs_sc[slot, HALO - 1:HALO - 1 + tm, :] * cw_ref[0:1, cs]
             + gs_sc[slot, HALO:HALO + tm, :] * cw_ref[1:2, cs]
             + gs_sc[slot, HALO + 1:HALO + 1 + tm, :] * cw_ref[2:3, cs])
        a = (z * (1.0 + lax.erf(z * math.sqrt(0.5))) * us_sc[slot]).astype(BF16)
        d = jnp.dot(a, wd_ref[cs, :], preferred_element_type=F32)
        return d if acc is None else acc + d

    ffn_in(0, 0)
    acc = None
    for c in range(n_chunks):
        if c + 1 < n_chunks:
            ffn_in(c + 1, (c + 1) % 2)
        acc = ffn_out(c, c % 2, acc)
    gt2 = mod_ref[0, 5:6, :]
    x2 = x1_ref[0] + gt2 * acc
    y_ref[0] = x2 * lax.rsqrt(jnp.mean(x2 * x2, axis=-1, keepdims=True) + EPS) * gf_ref[...]


def _ffn(x1, h2, mod3, w_ffn_in, conv_w, conv_b, g_final, w_down_half):
    b, n, _ = x1.shape
    tm = TM_FFN
    nt = n // tm
    rb = tm // HALO
    tok = pl.BlockSpec((1, tm, D_MODEL), lambda bi, i: (bi, i, 0))
    prv = pl.BlockSpec((1, HALO, D_MODEL), lambda bi, i: (bi, jnp.maximum(i * rb - 1, 0), 0))
    nxt = pl.BlockSpec((1, HALO, D_MODEL), lambda bi, i: (bi, jnp.minimum((i + 1) * rb, n // HALO - 1), 0))
    return pl.pallas_call(
        functools.partial(_ffn_kernel, tm=tm, nt=nt),
        grid=(b, nt),
        in_specs=[tok, tok, prv, nxt,
                  pl.BlockSpec((1, 6, D_MODEL), lambda bi, i: (bi, 0, 0)),
                  _const_spec((D_MODEL, 2 * D_FF)),
                  _const_spec((CONV_W, D_FF)), _const_spec((1, D_FF)),
                  _const_spec((1, D_MODEL)),
                  _const_spec((D_FF, D_MODEL))],
        out_specs=tok,
        out_shape=jax.ShapeDtypeStruct((b, n, D_MODEL), F32),
        scratch_shapes=[pltpu.VMEM((2, tm + 2 * HALO, FF_CHUNK), F32),
                        pltpu.VMEM((2, tm, FF_CHUNK), F32)],
        compiler_params=_cparams(("parallel", "parallel")),
        name="ffn",
    )(x1, h2, h2, h2, mod3, w_ffn_in, conv_w, conv_b, g_final, w_down_half)


def _rope_tables(n):
    t = jnp.arange(n, dtype=jnp.int32)
    row = (t // GRID_W).astype(F32)
    col = (t % GRID_W).astype(F32)
    inv = ROPE_THETA ** (-jnp.arange(0, ROPE_AXIS_DIM, 2, dtype=F32) / ROPE_AXIS_DIM)
    ar = row[:, None] * inv[None, :]
    ac = col[:, None] * inv[None, :]
    cos_t = jnp.concatenate([jnp.cos(ar), jnp.cos(ar), jnp.cos(ac), jnp.cos(ac)], axis=-1)
    sin_t = jnp.concatenate([-jnp.sin(ar), jnp.sin(ar), -jnp.sin(ac), jnp.sin(ac)], axis=-1)
    return cos_t, sin_t


def _trunk(x, mod, p):
    b, n, _ = x.shape
    mod3 = mod.reshape(b, 6, D_MODEL)
    cos_t, sin_t = _rope_tables(n)
    qa, ka, va, qb, kb, vb, ga, gb = _inproj(x, mod3, p['g_norm1'], p['g_qnorm'], p['g_knorm'],
                                             cos_t, sin_t, p['w_in'])
    oa = _gqa(qa, ka, va)
    ob = _diff(qb, kb, vb, p['bias_t'], p['cb'], p['lam4'], p['g_subln'])
    x1, h2 = _mid(x, oa, ob, ga, gb, mod3, p['g_norm2'], p['w_out'])
    return _ffn(x1, h2, mod3, p['w_ffn_in'], p['conv_w'], p['conv_b'], p['g_final'], p['w_down'])


def kernel(x_prompt, x_sample, c_prompt, c_sample, w_mod, b_mod, g_norm1, w_in, g_qnorm, g_knorm,
           lambda_q1, lambda_k1, lambda_q2, lambda_k2, g_subln, rel_bias, w_out, g_norm2,
           w_ffn_in, conv_w, conv_b, w_down, g_final):
    bp, bs = x_prompt.shape[0], x_sample.shape[0]
    rows = bp + bs
    pad = (-rows) % 8
    c_all = jnp.pad(jnp.concatenate([c_prompt, c_sample], axis=0), ((0, pad), (0, 0)))
    mod = _mod(c_all, w_mod[0], b_mod)

    p = {
        'g_norm1': g_norm1, 'g_qnorm': g_qnorm, 'g_knorm': g_knorm, 'g_subln': g_subln,
        'g_norm2': g_norm2, 'g_final': g_final.reshape(1, D_MODEL),
        'w_in': w_in[0].astype(BF16), 'w_out': w_out[0].astype(BF16),
        'w_ffn_in': w_ffn_in[0].astype(BF16), 'w_down': (w_down[0] * 0.5).astype(BF16),
        'conv_w': conv_w[0], 'conv_b': conv_b,
        'lam4': jnp.concatenate([lambda_q1, lambda_k1, lambda_q2, lambda_k2], axis=0),
        'bias_t': _bias_tiles(rel_bias, T_B),
        'cb': jnp.stack([rel_bias[N_BUCKETS // 2 - 1], rel_bias[N_BUCKETS - 1]], axis=1) * LOG2E,
    }
    y_prompt = _trunk(x_prompt, mod[:bp], p)
    y_sample = _trunk(x_sample, mod[bp:rows], p)
    return (y_prompt, y_sample)
```

```python
import functools
import math

import jax
import jax.numpy as jnp
from jax import lax
from jax.experimental import pallas as pl
from jax.experimental.pallas import tpu as pltpu

F32 = jnp.float32
BF16 = jnp.bfloat16

D_MODEL = 1024
GRID_W = 64
EPS = 1e-6
HA_Q, HA_KV, G_A, HD_A = 8, 2, 4, 128
ROPE_AXIS_DIM = HD_A // 2
ROPE_THETA = 10000.0
HB, DH_B, DV_B = 8, 64, 128
N_BUCKETS, MAX_DIST = 32, 128
D_FF = 2816
CONV_W = 3
LAMBDA_INIT = 0.8 - 0.6 * math.exp(-0.3 * 0)
LOG2E = 1.4426950408889634
SCALE_A = HD_A ** -0.5 * LOG2E
SCALE_B = DH_B ** -0.5 * LOG2E
NEG_BIG = -0.7 * float(jnp.finfo(jnp.float32).max)

OFF_QA, OFF_KA, OFF_VA = 0, 1024, 1280
OFF_QB, OFF_KB, OFF_VB = 1536, 2560, 3584
OFF_GA, OFF_GB = 4608, 5632
N_IN = 6656

LANES = 128
VMEM_LIMIT = 56 * 1024 * 1024

TM_IN = 512
TM_MID = 512
TM_FFN = 256
HALO = 16
FF_CHUNK = 256
T_A = 512
T_B = 512


def _cparams(sem):
    return pltpu.CompilerParams(dimension_semantics=sem, vmem_limit_bytes=VMEM_LIMIT)


def _const_spec(shape):
    nd = len(shape)
    return pl.BlockSpec(shape, lambda *_: (0,) * nd, pipeline_mode=pl.Buffered(1))


def _mod_kernel(c_ref, w_ref, b_ref, o_ref):
    c = c_ref[...]
    sc = c * jax.nn.sigmoid(c)
    o_ref[...] = jnp.dot(sc, w_ref[...], preferred_element_type=F32,
                         precision=lax.Precision.HIGHEST) + b_ref[...]


def _mod(c, w_mod, b_mod):
    rows = c.shape[0]
    n_out = w_mod.shape[1]
    blk = 1024
    return pl.pallas_call(
        _mod_kernel,
        grid=(n_out // blk,),
        in_specs=[pl.BlockSpec((rows, D_MODEL), lambda j: (0, 0)),
                  pl.BlockSpec((D_MODEL, blk), lambda j: (0, j)),
                  pl.BlockSpec((1, blk), lambda j: (0, j))],
        out_specs=pl.BlockSpec((rows, blk), lambda j: (0, j)),
        out_shape=jax.ShapeDtypeStruct((rows, n_out), F32),
        compiler_params=_cparams(("parallel",)),
        name="mod",
    )(c, w_mod, b_mod)


def _bias_kernel(tab_ref, o_ref, *, t):
    cls = pl.program_id(0)
    row = lax.broadcasted_iota(jnp.int32, (t, t), 0)
    col = lax.broadcasted_iota(jnp.int32, (t, t), 1)
    rel = (cls - 1) * t + col - row
    nb = N_BUCKETS // 2
    max_exact = nb // 2
    ret = jnp.where(rel > 0, nb, 0)
    n = jnp.abs(rel)
    large = max_exact + (jnp.log(jnp.maximum(n, 1).astype(F32) / max_exact)
                         / math.log(MAX_DIST / max_exact) * (nb - max_exact)).astype(jnp.int32)
    large = jnp.minimum(large, nb - 1)
    bucket = ret + jnp.where(n < max_exact, n, large)
    for h in range(HB):
        acc = jnp.zeros((t, t), F32)
        for b in range(N_BUCKETS):
            acc = jnp.where(bucket == b, tab_ref[b, h], acc)
        o_ref[h, 0] = acc * LOG2E


def _bias_tiles(rel_bias, t):
    return pl.pallas_call(
        functools.partial(_bias_kernel, t=t),
        grid=(3,),
        in_specs=[pl.BlockSpec(memory_space=pltpu.SMEM)],
        out_specs=pl.BlockSpec((HB, 1, t, t), lambda c: (0, c, 0, 0)),
        out_shape=jax.ShapeDtypeStruct((HB, 3, t, t), F32),
        compiler_params=_cparams(("parallel",)),
        name="bias_tiles",
    )(rel_bias)


def _inproj_kernel(x_ref, mod_ref, g1_ref, gq_ref, gk_ref, cos_ref, sin_ref, w_ref,
                   qa_ref, ka_ref, va_ref, qb_ref, kb_ref, vb_ref, ga_ref, gb_ref, *, tm):
    x = x_ref[0]
    y = x * lax.rsqrt(jnp.mean(x * x, axis=-1, keepdims=True) + EPS)
    sh1 = mod_ref[0, 0:1, :]
    sc1 = mod_ref[0, 1:2, :]
    hb = ((y * g1_ref[...]) * (1.0 + sc1) + sh1).astype(BF16)

    cos = cos_ref[...]
    sin = sin_ref[...]
    lane = lax.broadcasted_iota(jnp.int32, (tm, LANES), 1)
    low_half = (lane & 32) == 0

    def proj(c0, width):
        return jnp.dot(hb, w_ref[:, c0:c0 + width], preferred_element_type=F32)

    def norm_rope(p, g, scale):
        yn = p * lax.rsqrt(jnp.mean(p * p, axis=-1, keepdims=True) + EPS) * g
        partner = jnp.where(low_half, pltpu.roll(yn, 96, 1), pltpu.roll(yn, 32, 1))
        r = yn * cos + partner * sin
        return r * scale if scale is not None else r

    gq = gq_ref[...]
    gk = gk_ref[...]
    for c in range(2):
        p = proj(OFF_QA + c * 512, 512)
        for j in range(4):
            qa_ref[0, c * 4 + j] = norm_rope(p[:, j * LANES:(j + 1) * LANES], gq, SCALE_A).astype(BF16)
    p = proj(OFF_KA, 512)
    for j in range(2):
        ka_ref[0, j] = norm_rope(p[:, j * LANES:(j + 1) * LANES], gk, None).astype(BF16)
        va_ref[0, j] = p[:, (2 + j) * LANES:(3 + j) * LANES].astype(BF16)
    for c in range(2):
        p = proj(OFF_QB + c * 512, 512)
        for j in range(4):
            qb_ref[0, c * 4 + j] = (p[:, j * LANES:(j + 1) * LANES] * SCALE_B).astype(BF16)
    for c in range(2):
        p = proj(OFF_KB + c * 512, 512)
        for j in range(4):
            kb_ref[0, c * 4 + j] = p[:, j * LANES:(j + 1) * LANES].astype(BF16)
    for c in range(2):
        p = proj(OFF_VB + c * 512, 512)
        for j in range(4):
            vb_ref[0, c * 4 + j] = p[:, j * LANES:(j + 1) * LANES].astype(BF16)
    for c in range(2):
        ga_ref[0, :, c * 512:(c + 1) * 512] = jax.nn.sigmoid(proj(OFF_GA + c * 512, 512)).astype(BF16)
    for c in range(2):
        gb_ref[0, :, c * 512:(c + 1) * 512] = jax.nn.sigmoid(proj(OFF_GB + c * 512, 512)).astype(BF16)


def _inproj(x, mod3, g1, gq, gk, cos_t, sin_t, w_in):
    b, n, _ = x.shape
    tm = TM_IN
    hm = lambda heads: pl.BlockSpec((1, heads, tm, LANES), lambda bi, i: (bi, 0, i, 0))
    tok = pl.BlockSpec((1, tm, D_MODEL), lambda bi, i: (bi, i, 0))
    hshape = lambda heads: jax.ShapeDtypeStruct((b, heads, n, LANES), BF16)
    return pl.pallas_call(
        functools.partial(_inproj_kernel, tm=tm),
        grid=(b, n // tm),
        in_specs=[tok,
                  pl.BlockSpec((1, 6, D_MODEL), lambda bi, i: (bi, 0, 0)),
                  _const_spec((1, D_MODEL)), _const_spec((1, LANES)), _const_spec((1, LANES)),
                  pl.BlockSpec((tm, LANES), lambda bi, i: (i, 0)),
                  pl.BlockSpec((tm, LANES), lambda bi, i: (i, 0)),
                  _const_spec((D_MODEL, N_IN))],
        out_specs=[hm(HA_Q), hm(HA_KV), hm(HA_KV), hm(HB), hm(HB), hm(HB), tok, tok],
        out_shape=[hshape(HA_Q), hshape(HA_KV), hshape(HA_KV), hshape(HB), hshape(HB), hshape(HB),
                   jax.ShapeDtypeStruct((b, n, D_MODEL), BF16), jax.ShapeDtypeStruct((b, n, D_MODEL), BF16)],
        compiler_params=_cparams(("parallel", "parallel")),
        name="inproj",
    )(x, mod3, g1, gq, gk, cos_t, sin_t, w_in)


def _stage_scores(s, s_slot, mc_slot):
    s_slot[...] = s
    mc_slot[...] = jnp.broadcast_to(jnp.max(s, axis=1, keepdims=True), mc_slot.shape)


def _stage_update(s_slot, mc_slot, shift, m_ref, accl_ref, v, tk):
    m_prev = m_ref[...]
    if shift is None:
        m_next = jnp.maximum(m_prev, mc_slot[...])
        sub = m_next
    else:
        m_next = jnp.maximum(m_prev, mc_slot[...] + shift)
        sub = m_next - shift
    p = jnp.exp2(s_slot[...] - jnp.concatenate([sub] * (tk // LANES), axis=1)).astype(BF16)
    alpha = jnp.exp2(m_prev - m_next)
    v_ones = jnp.concatenate([v, jnp.ones((tk, LANES), BF16)], axis=1)
    m_ref[...] = m_next
    accl_ref[...] = (accl_ref[...] * jnp.concatenate([alpha, alpha], axis=1)
                     + jnp.dot(p, v_ones, preferred_element_type=F32))


def _pipelined(n_units, scores_fn, update_fn):
    scores_fn(0, 0)
    for u in range(n_units):
        if u + 1 < n_units:
            scores_fn(u + 1, (u + 1) % 2)
        update_fn(u, u % 2)


def _gqa_kernel(q_ref, k_ref, v_ref, sa_ref, o_ref, m_sc, accl_sc, s_sc, mc_sc, *, tq, ts, n_sub, nk):
    ki = pl.program_id(2)

    @pl.when(ki == 0)
    def _():
        m_sc[...] = jnp.full(m_sc.shape, NEG_BIG, F32)
        accl_sc[...] = jnp.zeros(accl_sc.shape, F32)

    def scores(u, slot):
        sub, h = divmod(u, HA_Q)
        k = k_ref[0, h // G_A, sub * ts:(sub + 1) * ts]
        s = lax.dot_general(q_ref[0, h], k, (((1,), (1,)), ((), ())), preferred_element_type=F32)
        _stage_scores(s, s_sc.at[slot], mc_sc.at[slot])

    def update(u, slot):
        sub, h = divmod(u, HA_Q)
        _stage_update(s_sc.at[slot], mc_sc.at[slot], None, m_sc.at[h], accl_sc.at[h],
                      v_ref[0, h // G_A, sub * ts:(sub + 1) * ts], ts)

    _pipelined(n_sub * HA_Q, scores, update)

    @pl.when(ki == nk - 1)
    def _():
        for h in range(HA_Q):
            cols = slice(h * HD_A, (h + 1) * HD_A)
            o = accl_sc[h, :, :HD_A] / accl_sc[h, :, HD_A:]
            o_ref[0, :, cols] = (sa_ref[0, :, cols].astype(F32) * o).astype(BF16)


def _gqa(qa, ka, va, sa):
    b, _, n, _ = qa.shape
    tq, ts, n_sub = T_A, 2 * T_A, 1
    tk = ts * n_sub
    nq, nk = n // tq, n // tk
    return pl.pallas_call(
        functools.partial(_gqa_kernel, tq=tq, ts=ts, n_sub=n_sub, nk=nk),
        grid=(b, nq, nk),
        in_specs=[pl.BlockSpec((1, HA_Q, tq, HD_A), lambda bi, qi, ki: (bi, 0, qi, 0)),
                  pl.BlockSpec((1, HA_KV, tk, HD_A), lambda bi, qi, ki: (bi, 0, ki, 0)),
                  pl.BlockSpec((1, HA_KV, tk, HD_A), lambda bi, qi, ki: (bi, 0, ki, 0)),
                  pl.BlockSpec((1, tq, D_MODEL), lambda bi, qi, ki: (bi, qi, 0))],
        out_specs=pl.BlockSpec((1, tq, D_MODEL), lambda bi, qi, ki: (bi, qi, 0)),
        out_shape=jax.ShapeDtypeStruct((b, n, D_MODEL), BF16),
        scratch_shapes=[pltpu.VMEM((HA_Q, tq, LANES), F32),
                        pltpu.VMEM((HA_Q, tq, HD_A + LANES), F32),
                        pltpu.VMEM((2, tq, ts), F32),
                        pltpu.VMEM((2, tq, LANES), F32)],
        compiler_params=_cparams(("parallel", "parallel", "arbitrary")),
        name="gqa",
    )(qa, ka, va, sa)


def _diff_kernel(cb_ref, lam_ref, gs_ref, q_ref, k_ref, v_ref, bias_ref, ga_ref, sb_ref, o_ref,
                 q2_sc, m_sc, accl_sc, s_sc, mc_sc, *, t, nk):
    qi = pl.program_id(1)
    ki = pl.program_id(2)

    @pl.when(ki == 0)
    def _():
        m_sc[...] = jnp.full(m_sc.shape, NEG_BIG, F32)
        accl_sc[...] = jnp.zeros(accl_sc.shape, F32)
        first = lax.broadcasted_iota(jnp.int32, (t, LANES), 1) < DH_B
        zero = jnp.zeros((t, LANES), BF16)
        for h in range(HB):
            q = q_ref[0, h]
            q2_sc[h, :t] = jnp.where(first, q, zero)
            q2_sc[h, t:] = jnp.where(first, zero, q)

    def heads(near):
        def scores(u, slot):
            h, half = divmod(u, 2)
            s = lax.dot_general(q2_sc[h, half * t:(half + 1) * t], k_ref[0, h], (((1,), (1,)), ((), ())),
                                preferred_element_type=F32)
            if near:
                s = s + bias_ref[h, 0]
            _stage_scores(s, s_sc.at[slot], mc_sc.at[slot])

        def update(u, slot):
            h, half = divmod(u, 2)
            rs = slice(half * t, (half + 1) * t)
            shift = None if near else jnp.where(ki > qi, cb_ref[h, 1], cb_ref[h, 0])
            _stage_update(s_sc.at[slot], mc_sc.at[slot], shift, m_sc.at[h, rs], accl_sc.at[h, rs], v_ref[0, h], t)

        _pipelined(2 * HB, scores, update)

    near_diag = jnp.abs(ki - qi) <= 1

    @pl.when(near_diag)
    def _():
        heads(True)

    @pl.when(jnp.logical_not(near_diag))
    def _():
        heads(False)

    @pl.when(ki == nk - 1)
    def _():
        lam = (jnp.exp(jnp.sum(lam_ref[0:1, :] * lam_ref[1:2, :], axis=-1, keepdims=True))
               - jnp.exp(jnp.sum(lam_ref[2:3, :] * lam_ref[3:4, :], axis=-1, keepdims=True))
               + LAMBDA_INIT)
        gs = gs_ref[...]
        for h in range(HB):
            o12 = accl_sc[h, :, :DV_B] / accl_sc[h, :, DV_B:]
            o = o12[:t] - lam * o12[t:]
            o = o * lax.rsqrt(jnp.mean(o * o, axis=-1, keepdims=True) + EPS) * gs * (1.0 - LAMBDA_INIT)
            cols = slice(h * DV_B, (h + 1) * DV_B)
            merged = ga_ref[0, :, cols].astype(F32) + sb_ref[0, :, cols].astype(F32) * o
            o_ref[0, :, cols] = merged.astype(BF16)


def _diff(qb, kb, vb, bias_t, cb, lam4, gs, gated_a, sb):
    b, _, n, _ = qb.shape
    t = T_B
    nq = nk = n // t
    hm = lambda idx: pl.BlockSpec((1, HB, t, LANES), idx)
    return pl.pallas_call(
        functools.partial(_diff_kernel, t=t, nk=nk),
        grid=(b, nq, nk),
        in_specs=[pl.BlockSpec(memory_space=pltpu.SMEM),
                  _const_spec((4, DH_B)), _const_spec((1, DV_B)),
                  hm(lambda bi, qi, ki: (bi, 0, qi, 0)),
                  hm(lambda bi, qi, ki: (bi, 0, ki, 0)),
                  hm(lambda bi, qi, ki: (bi, 0, ki, 0)),
                  pl.BlockSpec((HB, 1, t, t), lambda bi, qi, ki: (0, jnp.clip(ki - qi, -1, 1) + 1, 0, 0)),
                  pl.BlockSpec((1, t, D_MODEL), lambda bi, qi, ki: (bi, qi, 0)),
                  pl.BlockSpec((1, t, D_MODEL), lambda bi, qi, ki: (bi, qi, 0))],
        out_specs=pl.BlockSpec((1, t, D_MODEL), lambda bi, qi, ki: (bi, qi, 0)),
        out_shape=jax.ShapeDtypeStruct((b, n, D_MODEL), BF16),
        scratch_shapes=[pltpu.VMEM((HB, 2 * t, LANES), BF16),
                        pltpu.VMEM((HB, 2 * t, LANES), F32),
                        pltpu.VMEM((HB, 2 * t, DV_B + LANES), F32),
                        pltpu.VMEM((2, t, t), F32),
                        pltpu.VMEM((2, t, LANES), F32)],
        compiler_params=_cparams(("parallel", "parallel", "arbitrary")),
        name="diff",
    )(cb, lam4, gs, qb, kb, vb, bias_t, gated_a, sb)


def _mid_kernel(x_ref, merged_ref, mod_ref, g2_ref, wo_ref, x1_ref, h2_ref):
    merged = merged_ref[0]
    gt1 = mod_ref[0, 2:3, :]
    sh2 = mod_ref[0, 3:4, :]
    sc2 = mod_ref[0, 4:5, :]
    x1 = x_ref[0] + gt1 * jnp.dot(merged, wo_ref[...], preferred_element_type=F32)
    x1_ref[0] = x1
    y = x1 * lax.rsqrt(jnp.mean(x1 * x1, axis=-1, keepdims=True) + EPS)
    h2_ref[0] = ((y * g2_ref[...]) * (1.0 + sc2) + sh2).astype(BF16)


def _mid(x, merged, mod3, g2, w_out):
    b, n, _ = x.shape
    tm = TM_MID
    tok = pl.BlockSpec((1, tm, D_MODEL), lambda bi, i: (bi, i, 0))
    return pl.pallas_call(
        _mid_kernel,
        grid=(b, n // tm),
        in_specs=[tok, tok,
                  pl.BlockSpec((1, 6, D_MODEL), lambda bi, i: (bi, 0, 0)),
                  _const_spec((1, D_MODEL)),
                  _const_spec((D_MODEL, D_MODEL))],
        out_specs=[tok, tok],
        out_shape=[jax.ShapeDtypeStruct((b, n, D_MODEL), F32),
                   jax.ShapeDtypeStruct((b, n, D_MODEL), BF16)],
        compiler_params=_cparams(("parallel", "parallel")),
        name="mid",
    )(x, merged, mod3, g2, w_out)


def _ffn_kernel(x1_ref, h_ref, hp_ref, hn_ref, mod_ref, wf_ref, cw_ref, cbias_ref, gf_ref, wd_ref,
                y_ref, gs_sc, us_sc, *, tm, nt):
    i = pl.program_id(1)
    rows = tm + 2 * HALO
    h2 = jnp.concatenate([hp_ref[0], h_ref[0], hn_ref[0]], axis=0)
    h2m = h_ref[0]
    ridx = lax.broadcasted_iota(jnp.int32, (rows, 1), 0)
    keep = jnp.logical_and(jnp.logical_or(i > 0, ridx >= HALO), jnp.logical_or(i < nt - 1, ridx < HALO + tm))
    n_chunks = D_FF // FF_CHUNK

    def ffn_in(c, slot):
        cs = slice(c * FF_CHUNK, (c + 1) * FF_CHUNK)
        gcs = slice(D_FF + c * FF_CHUNK, D_FF + (c + 1) * FF_CHUNK)
        us_sc[slot] = jnp.dot(h2m, wf_ref[:, cs], preferred_element_type=F32)
        gs_sc[slot] = jnp.where(keep, jnp.dot(h2, wf_ref[:, gcs], preferred_element_type=F32), 0.0)

    def ffn_out(c, slot, acc):
        cs = slice(c * FF_CHUNK, (c + 1) * FF_CHUNK)
        z = (cbias_ref[:, cs] + gs_sc[slot, HALO - 1:HALO - 1 + tm, :] * cw_ref[0:1, cs]
             + gs_sc[slot, HALO:HALO + tm, :] * cw_ref[1:2, cs]
             + gs_sc[slot, HALO + 1:HALO + 1 + tm, :] * cw_ref[2:3, cs])
        a = (z * (1.0 + lax.erf(z * math.sqrt(0.5))) * us_sc[slot]).astype(BF16)
        d = jnp.dot(a, wd_ref[cs, :], preferred_element_type=F32)
        return d if acc is None else acc + d

    ffn_in(0, 0)
    acc = None
    for c in range(n_chunks):
        if c + 1 < n_chunks:
            ffn_in(c + 1, (c + 1) % 2)
        acc = ffn_out(c, c % 2, acc)
    gt2 = mod_ref[0, 5:6, :]
    x2 = x1_ref[0] + gt2 * acc
    y_ref[0] = x2 * lax.rsqrt(jnp.mean(x2 * x2, axis=-1, keepdims=True) + EPS) * gf_ref[...]


def _ffn(x1, h2, mod3, w_ffn_in, conv_w, conv_b, g_final, w_down_half):
    b, n, _ = x1.shape
    tm = TM_FFN
    nt = n // tm
    rb = tm // HALO
    tok = pl.BlockSpec((1, tm, D_MODEL), lambda bi, i: (bi, i, 0))
    prv = pl.BlockSpec((1, HALO, D_MODEL), lambda bi, i: (bi, jnp.maximum(i * rb - 1, 0), 0))
    nxt = pl.BlockSpec((1, HALO, D_MODEL), lambda bi, i: (bi, jnp.minimum((i + 1) * rb, n // HALO - 1), 0))
    return pl.pallas_call(
        functools.partial(_ffn_kernel, tm=tm, nt=nt),
        grid=(b, nt),
        in_specs=[tok, tok, prv, nxt,
                  pl.BlockSpec((1, 6, D_MODEL), lambda bi, i: (bi, 0, 0)),
                  _const_spec((D_MODEL, 2 * D_FF)),
                  _const_spec((CONV_W, D_FF)), _const_spec((1, D_FF)),
                  _const_spec((1, D_MODEL)),
                  _const_spec((D_FF, D_MODEL))],
        out_specs=tok,
        out_shape=jax.ShapeDtypeStruct((b, n, D_MODEL), F32),
        scratch_shapes=[pltpu.VMEM((2, tm + 2 * HALO, FF_CHUNK), F32),
                        pltpu.VMEM((2, tm, FF_CHUNK), F32)],
        compiler_params=_cparams(("parallel", "parallel")),
        name="ffn",
    )(x1, h2, h2, h2, mod3, w_ffn_in, conv_w, conv_b, g_final, w_down_half)


def _rope_tables(n):
    t = jnp.arange(n, dtype=jnp.int32)
    row = (t // GRID_W).astype(F32)
    col = (t % GRID_W).astype(F32)
    inv = ROPE_THETA ** (-jnp.arange(0, ROPE_AXIS_DIM, 2, dtype=F32) / ROPE_AXIS_DIM)
    ar = row[:, None] * inv[None, :]
    ac = col[:, None] * inv[None, :]
    cos_t = jnp.concatenate([jnp.cos(ar), jnp.cos(ar), jnp.cos(ac), jnp.cos(ac)], axis=-1)
    sin_t = jnp.concatenate([-jnp.sin(ar), jnp.sin(ar), -jnp.sin(ac), jnp.sin(ac)], axis=-1)
    return cos_t, sin_t


def _trunk(x, mod, p):
    b, n, _ = x.shape
    mod3 = mod.reshape(b, 6, D_MODEL)
    cos_t, sin_t = _rope_tables(n)
    qa, ka, va, qb, kb, vb, ga, gb = _inproj(x, mod3, p['g_norm1'], p['g_qnorm'], p['g_knorm'],
                                             cos_t, sin_t, p['w_in'])
    gated_a = _gqa(qa, ka, va, ga)
    merged = _diff(qb, kb, vb, p['bias_t'], p['cb'], p['lam4'], p['g_subln'], gated_a, gb)
    x1, h2 = _mid(x, merged, mod3, p['g_norm2'], p['w_out'])
    return _ffn(x1, h2, mod3, p['w_ffn_in'], p['conv_w'], p['conv_b'], p['g_final'], p['w_down'])


def kernel(x_prompt, x_sample, c_prompt, c_sample, w_mod, b_mod, g_norm1, w_in, g_qnorm, g_knorm,
           lambda_q1, lambda_k1, lambda_q2, lambda_k2, g_subln, rel_bias, w_out, g_norm2,
           w_ffn_in, conv_w, conv_b, w_down, g_final):
    bp, bs = x_prompt.shape[0], x_sample.shape[0]
    rows = bp + bs
    pad = (-rows) % 8
    c_all = jnp.pad(jnp.concatenate([c_prompt, c_sample], axis=0), ((0, pad), (0, 0)))
    mod = _mod(c_all, w_mod[0], b_mod)

    p = {
        'g_norm1': g_norm1, 'g_qnorm': g_qnorm, 'g_knorm': g_knorm, 'g_subln': g_subln,
        'g_norm2': g_norm2, 'g_final': g_final.reshape(1, D_MODEL),
        'w_in': w_in[0].astype(BF16), 'w_out': w_out[0].astype(BF16),
        'w_ffn_in': w_ffn_in[0].astype(BF16), 'w_down': (w_down[0] * 0.5).astype(BF16),
        'conv_w': conv_w[0], 'conv_b': conv_b,
        'lam4': jnp.concatenate([lambda_q1, lambda_k1, lambda_q2, lambda_k2], axis=0),
        'bias_t': _bias_tiles(rel_bias, T_B),
        'cb': jnp.stack([rel_bias[N_BUCKETS // 2 - 1], rel_bias[N_BUCKETS - 1]], axis=1) * LOG2E,
    }
    y_prompt = _trunk(x_prompt, mod[:bp], p)
    y_sample = _trunk(x_sample, mod[bp:rows], p)
    return (y_prompt, y_sample)
```

```python
import functools
import math

import jax
import jax.numpy as jnp
from jax import lax
from jax.experimental import pallas as pl
from jax.experimental.pallas import tpu as pltpu

F32 = jnp.float32
BF16 = jnp.bfloat16

D_MODEL = 1024
GRID_W = 64
EPS = 1e-6
HA_Q, HA_KV, G_A, HD_A = 8, 2, 4, 128
ROPE_AXIS_DIM = HD_A // 2
ROPE_THETA = 10000.0
HB, DH_B, DV_B = 8, 64, 128
N_BUCKETS, MAX_DIST = 32, 128
D_FF = 2816
CONV_W = 3
LAMBDA_INIT = 0.8 - 0.6 * math.exp(-0.3 * 0)
LOG2E = 1.4426950408889634
SCALE_A = HD_A ** -0.5 * LOG2E
SCALE_B = DH_B ** -0.5 * LOG2E
NEG_BIG = -0.7 * float(jnp.finfo(jnp.float32).max)

OFF_QA, OFF_KA, OFF_VA = 0, 1024, 1280
OFF_QB, OFF_KB, OFF_VB = 1536, 2560, 3584
OFF_GA, OFF_GB = 4608, 5632
N_IN = 6656

LANES = 128
VMEM_LIMIT = 56 * 1024 * 1024

TM_IN = 512
PROJ_CHUNK = 512
TM_MID = 512
TM_FFN = 512
HALO = 16
FF_CHUNK = 256
T_A = 512
T_B = 512


def _cparams(sem):
    return pltpu.CompilerParams(dimension_semantics=sem, vmem_limit_bytes=VMEM_LIMIT)


def _const_spec(shape):
    nd = len(shape)
    return pl.BlockSpec(shape, lambda *_: (0,) * nd, pipeline_mode=pl.Buffered(1))


def _mod_kernel(c_ref, w_ref, b_ref, o_ref):
    c = c_ref[...]
    sc = c * jax.nn.sigmoid(c)
    o_ref[...] = jnp.dot(sc, w_ref[...], preferred_element_type=F32,
                         precision=lax.Precision.HIGHEST) + b_ref[...]


def _mod(c, w_mod, b_mod):
    rows = c.shape[0]
    n_out = w_mod.shape[1]
    blk = 1024
    return pl.pallas_call(
        _mod_kernel,
        grid=(n_out // blk,),
        in_specs=[pl.BlockSpec((rows, D_MODEL), lambda j: (0, 0)),
                  pl.BlockSpec((D_MODEL, blk), lambda j: (0, j)),
                  pl.BlockSpec((1, blk), lambda j: (0, j))],
        out_specs=pl.BlockSpec((rows, blk), lambda j: (0, j)),
        out_shape=jax.ShapeDtypeStruct((rows, n_out), F32),
        compiler_params=_cparams(("parallel",)),
        name="mod",
    )(c, w_mod, b_mod)


def _bias_kernel(tab_ref, o_ref, *, t):
    cls = pl.program_id(0)
    row = lax.broadcasted_iota(jnp.int32, (t, t), 0)
    col = lax.broadcasted_iota(jnp.int32, (t, t), 1)
    rel = (cls - 1) * t + col - row
    nb = N_BUCKETS // 2
    max_exact = nb // 2
    ret = jnp.where(rel > 0, nb, 0)
    n = jnp.abs(rel)
    large = max_exact + (jnp.log(jnp.maximum(n, 1).astype(F32) / max_exact)
                         / math.log(MAX_DIST / max_exact) * (nb - max_exact)).astype(jnp.int32)
    large = jnp.minimum(large, nb - 1)
    bucket = ret + jnp.where(n < max_exact, n, large)
    for h in range(HB):
        acc = jnp.zeros((t, t), F32)
        for b in range(N_BUCKETS):
            acc = jnp.where(bucket == b, tab_ref[b, h], acc)
        o_ref[h, 0] = acc * LOG2E


def _bias_tiles(rel_bias, t):
    return pl.pallas_call(
        functools.partial(_bias_kernel, t=t),
        grid=(3,),
        in_specs=[pl.BlockSpec(memory_space=pltpu.SMEM)],
        out_specs=pl.BlockSpec((HB, 1, t, t), lambda c: (0, c, 0, 0)),
        out_shape=jax.ShapeDtypeStruct((HB, 3, t, t), F32),
        compiler_params=_cparams(("parallel",)),
        name="bias_tiles",
    )(rel_bias)


def _inproj_kernel(x_ref, mod_ref, g1_ref, gq_ref, gk_ref, cos_ref, sin_ref, w_ref,
                   qa_ref, ka_ref, va_ref, qb_ref, kb_ref, vb_ref, ga_ref, gb_ref, *, tm):
    x = x_ref[0]
    y = x * lax.rsqrt(jnp.mean(x * x, axis=-1, keepdims=True) + EPS)
    sh1 = mod_ref[0, 0:1, :]
    sc1 = mod_ref[0, 1:2, :]
    hb = ((y * g1_ref[...]) * (1.0 + sc1) + sh1).astype(BF16)

    cos = cos_ref[...]
    sin = sin_ref[...]
    lane = lax.broadcasted_iota(jnp.int32, (tm, LANES), 1)
    low_half = (lane & 32) == 0

    def proj(c0, width):
        return jnp.dot(hb, w_ref[:, c0:c0 + width], preferred_element_type=F32)

    def norm_rope(p, g, scale):
        yn = p * lax.rsqrt(jnp.mean(p * p, axis=-1, keepdims=True) + EPS) * g
        partner = jnp.where(low_half, pltpu.roll(yn, 96, 1), pltpu.roll(yn, 32, 1))
        r = yn * cos + partner * sin
        return r * scale if scale is not None else r

    gq = gq_ref[...]
    gk = gk_ref[...]
    for c in range(2):
        ga_ref[0, :, c * PROJ_CHUNK:(c + 1) * PROJ_CHUNK] = jax.nn.sigmoid(proj(OFF_GA + c * PROJ_CHUNK, PROJ_CHUNK)).astype(BF16)
    for c in range(2):
        gb_ref[0, :, c * PROJ_CHUNK:(c + 1) * PROJ_CHUNK] = jax.nn.sigmoid(proj(OFF_GB + c * PROJ_CHUNK, PROJ_CHUNK)).astype(BF16)
    for c in range(2):
        p = proj(OFF_QA + c * PROJ_CHUNK, PROJ_CHUNK)
        for j in range(4):
            qa_ref[0, c * 4 + j] = norm_rope(p[:, j * LANES:(j + 1) * LANES], gq, SCALE_A).astype(BF16)
    p = proj(OFF_KA, PROJ_CHUNK)
    for j in range(2):
        ka_ref[0, j] = norm_rope(p[:, j * LANES:(j + 1) * LANES], gk, None).astype(BF16)
        va_ref[0, j] = p[:, (2 + j) * LANES:(3 + j) * LANES].astype(BF16)
    for c in range(2):
        p = proj(OFF_QB + c * PROJ_CHUNK, PROJ_CHUNK)
        for j in range(4):
            qb_ref[0, c * 4 + j] = (p[:, j * LANES:(j + 1) * LANES] * SCALE_B).astype(BF16)
    for c in range(2):
        p = proj(OFF_KB + c * PROJ_CHUNK, PROJ_CHUNK)
        for j in range(4):
            kb_ref[0, c * 4 + j] = p[:, j * LANES:(j + 1) * LANES].astype(BF16)
    for c in range(2):
        p = proj(OFF_VB + c * PROJ_CHUNK, PROJ_CHUNK)
        for j in range(4):
            vb_ref[0, c * 4 + j] = p[:, j * LANES:(j + 1) * LANES].astype(BF16)


def _inproj(x, mod3, g1, gq, gk, cos_t, sin_t, w_in):
    b, n, _ = x.shape
    tm = TM_IN
    hm = lambda heads: pl.BlockSpec((1, heads, tm, LANES), lambda bi, i: (bi, 0, i, 0))
    tok = pl.BlockSpec((1, tm, D_MODEL), lambda bi, i: (bi, i, 0))
    hshape = lambda heads: jax.ShapeDtypeStruct((b, heads, n, LANES), BF16)
    return pl.pallas_call(
        functools.partial(_inproj_kernel, tm=tm),
        grid=(b, n // tm),
        in_specs=[tok,
                  pl.BlockSpec((1, 6, D_MODEL), lambda bi, i: (bi, 0, 0)),
                  _const_spec((1, D_MODEL)), _const_spec((1, LANES)), _const_spec((1, LANES)),
                  pl.BlockSpec((tm, LANES), lambda bi, i: (i, 0)),
                  pl.BlockSpec((tm, LANES), lambda bi, i: (i, 0)),
                  _const_spec((D_MODEL, N_IN))],
        out_specs=[hm(HA_Q), hm(HA_KV), hm(HA_KV), hm(HB), hm(HB), hm(HB), tok, tok],
        out_shape=[hshape(HA_Q), hshape(HA_KV), hshape(HA_KV), hshape(HB), hshape(HB), hshape(HB),
                   jax.ShapeDtypeStruct((b, n, D_MODEL), BF16), jax.ShapeDtypeStruct((b, n, D_MODEL), BF16)],
        compiler_params=_cparams(("parallel", "parallel")),
        name="inproj",
    )(x, mod3, g1, gq, gk, cos_t, sin_t, w_in)


def _stage_scores(s, s_slot, mc_slot):
    s_slot[...] = s
    mc_slot[...] = jnp.broadcast_to(jnp.max(s, axis=1, keepdims=True), mc_slot.shape)


def _stage_update(s_slot, mc_slot, shift, m_ref, accl_ref, v, tk):
    m_prev = m_ref[...]
    if shift is None:
        m_next = jnp.maximum(m_prev, mc_slot[...])
        sub = m_next
    else:
        m_next = jnp.maximum(m_prev, mc_slot[...] + shift)
        sub = m_next - shift
    p = jnp.exp2(s_slot[...] - jnp.concatenate([sub] * (tk // LANES), axis=1)).astype(BF16)
    alpha = jnp.exp2(m_prev - m_next)
    v_ones = jnp.concatenate([v, jnp.ones((tk, LANES), BF16)], axis=1)
    m_ref[...] = m_next
    accl_ref[...] = (accl_ref[...] * jnp.concatenate([alpha, alpha], axis=1)
                     + jnp.dot(p, v_ones, preferred_element_type=F32))


def _pipelined(n_units, scores_fn, update_fn):
    scores_fn(0, 0)
    for u in range(n_units):
        if u + 1 < n_units:
            scores_fn(u + 1, (u + 1) % 2)
        update_fn(u, u % 2)


def _gqa_kernel(q_ref, k_ref, v_ref, sa_ref, o_ref, m_sc, accl_sc, s_sc, mc_sc, *, tq, ts, n_sub, nk):
    ki = pl.program_id(2)

    @pl.when(ki == 0)
    def _():
        m_sc[...] = jnp.full(m_sc.shape, NEG_BIG, F32)
        accl_sc[...] = jnp.zeros(accl_sc.shape, F32)

    def scores(u, slot):
        sub, h = divmod(u, HA_Q)
        k = k_ref[0, h // G_A, sub * ts:(sub + 1) * ts]
        s = lax.dot_general(q_ref[0, h], k, (((1,), (1,)), ((), ())), preferred_element_type=F32)
        _stage_scores(s, s_sc.at[slot], mc_sc.at[slot])

    def update(u, slot):
        sub, h = divmod(u, HA_Q)
        _stage_update(s_sc.at[slot], mc_sc.at[slot], None, m_sc.at[h], accl_sc.at[h],
                      v_ref[0, h // G_A, sub * ts:(sub + 1) * ts], ts)

    _pipelined(n_sub * HA_Q, scores, update)

    @pl.when(ki == nk - 1)
    def _():
        for h in range(HA_Q):
            cols = slice(h * HD_A, (h + 1) * HD_A)
            o = accl_sc[h, :, :HD_A] / accl_sc[h, :, HD_A:]
            o_ref[0, :, cols] = (sa_ref[0, :, cols].astype(F32) * o).astype(BF16)


def _gqa(qa, ka, va, sa):
    b, _, n, _ = qa.shape
    tq, ts, n_sub = T_A, 2 * T_A, 1
    tk = ts * n_sub
    nq, nk = n // tq, n // tk
    return pl.pallas_call(
        functools.partial(_gqa_kernel, tq=tq, ts=ts, n_sub=n_sub, nk=nk),
        grid=(b, nq, nk),
        in_specs=[pl.BlockSpec((1, HA_Q, tq, HD_A), lambda bi, qi, ki: (bi, 0, qi, 0)),
                  pl.BlockSpec((1, HA_KV, tk, HD_A), lambda bi, qi, ki: (bi, 0, ki, 0)),
                  pl.BlockSpec((1, HA_KV, tk, HD_A), lambda bi, qi, ki: (bi, 0, ki, 0)),
                  pl.BlockSpec((1, tq, D_MODEL), lambda bi, qi, ki: (bi, qi, 0))],
        out_specs=pl.BlockSpec((1, tq, D_MODEL), lambda bi, qi, ki: (bi, qi, 0)),
        out_shape=jax.ShapeDtypeStruct((b, n, D_MODEL), BF16),
        scratch_shapes=[pltpu.VMEM((HA_Q, tq, LANES), F32),
                        pltpu.VMEM((HA_Q, tq, HD_A + LANES), F32),
                        pltpu.VMEM((2, tq, ts), F32),
                        pltpu.VMEM((2, tq, LANES), F32)],
        compiler_params=_cparams(("parallel", "parallel", "arbitrary")),
        name="gqa",
    )(qa, ka, va, sa)


def _diff_kernel(cb_ref, lam_ref, gs_ref, q_ref, k_ref, v_ref, bias_ref, ga_ref, sb_ref, o_ref,
                 q2_sc, m_sc, accl_sc, s_sc, mc_sc, *, t, nk):
    qi = pl.program_id(1)
    ki = pl.program_id(2)

    @pl.when(ki == 0)
    def _():
        m_sc[...] = jnp.full(m_sc.shape, NEG_BIG, F32)
        accl_sc[...] = jnp.zeros(accl_sc.shape, F32)
        first = lax.broadcasted_iota(jnp.int32, (t, LANES), 1) < DH_B
        zero = jnp.zeros((t, LANES), BF16)
        for h in range(HB):
            q = q_ref[0, h]
            q2_sc[h, :t] = jnp.where(first, q, zero)
            q2_sc[h, t:] = jnp.where(first, zero, q)

    def heads(near):
        def scores(u, slot):
            h, half = divmod(u, 2)
            s = lax.dot_general(q2_sc[h, half * t:(half + 1) * t], k_ref[0, h], (((1,), (1,)), ((), ())),
                                preferred_element_type=F32)
            if near:
                s = s + bias_ref[h, 0]
            _stage_scores(s, s_sc.at[slot], mc_sc.at[slot])

        def update(u, slot):
            h, half = divmod(u, 2)
            rs = slice(half * t, (half + 1) * t)
            shift = None if near else jnp.where(ki > qi, cb_ref[h, 1], cb_ref[h, 0])
            _stage_update(s_sc.at[slot], mc_sc.at[slot], shift, m_sc.at[h, rs], accl_sc.at[h, rs], v_ref[0, h], t)

        _pipelined(2 * HB, scores, update)

    near_diag = jnp.abs(ki - qi) <= 1

    @pl.when(near_diag)
    def _():
        heads(True)

    @pl.when(jnp.logical_not(near_diag))
    def _():
        heads(False)

    @pl.when(ki == nk - 1)
    def _():
        lam = (jnp.exp(jnp.sum(lam_ref[0:1, :] * lam_ref[1:2, :], axis=-1, keepdims=True))
               - jnp.exp(jnp.sum(lam_ref[2:3, :] * lam_ref[3:4, :], axis=-1, keepdims=True))
               + LAMBDA_INIT)
        gs = gs_ref[...]
        for h in range(HB):
            o12 = accl_sc[h, :, :DV_B] / accl_sc[h, :, DV_B:]
            o = o12[:t] - lam * o12[t:]
            o = o * lax.rsqrt(jnp.mean(o * o, axis=-1, keepdims=True) + EPS) * gs * (1.0 - LAMBDA_INIT)
            cols = slice(h * DV_B, (h + 1) * DV_B)
            merged = ga_ref[0, :, cols].astype(F32) + sb_ref[0, :, cols].astype(F32) * o
            o_ref[0, :, cols] = merged.astype(BF16)


def _diff(qb, kb, vb, bias_t, cb, lam4, gs, gated_a, sb):
    b, _, n, _ = qb.shape
    t = T_B
    nq = nk = n // t
    hm = lambda idx: pl.BlockSpec((1, HB, t, LANES), idx)
    return pl.pallas_call(
        functools.partial(_diff_kernel, t=t, nk=nk),
        grid=(b, nq, nk),
        in_specs=[pl.BlockSpec(memory_space=pltpu.SMEM),
                  _const_spec((4, DH_B)), _const_spec((1, DV_B)),
                  hm(lambda bi, qi, ki: (bi, 0, qi, 0)),
                  hm(lambda bi, qi, ki: (bi, 0, ki, 0)),
                  hm(lambda bi, qi, ki: (bi, 0, ki, 0)),
                  pl.BlockSpec((HB, 1, t, t), lambda bi, qi, ki: (0, jnp.clip(ki - qi, -1, 1) + 1, 0, 0)),
                  pl.BlockSpec((1, t, D_MODEL), lambda bi, qi, ki: (bi, qi, 0)),
                  pl.BlockSpec((1, t, D_MODEL), lambda bi, qi, ki: (bi, qi, 0))],
        out_specs=pl.BlockSpec((1, t, D_MODEL), lambda bi, qi, ki: (bi, qi, 0)),
        out_shape=jax.ShapeDtypeStruct((b, n, D_MODEL), BF16),
        scratch_shapes=[pltpu.VMEM((HB, 2 * t, LANES), BF16),
                        pltpu.VMEM((HB, 2 * t, LANES), F32),
                        pltpu.VMEM((HB, 2 * t, DV_B + LANES), F32),
                        pltpu.VMEM((2, t, t), F32),
                        pltpu.VMEM((2, t, LANES), F32)],
        compiler_params=_cparams(("parallel", "parallel", "arbitrary")),
        name="diff",
    )(cb, lam4, gs, qb, kb, vb, bias_t, gated_a, sb)


def _mid_kernel(x_ref, merged_ref, mod_ref, g2_ref, wo_ref, x1_ref, h2_ref):
    merged = merged_ref[0]
    gt1 = mod_ref[0, 2:3, :]
    sh2 = mod_ref[0, 3:4, :]
    sc2 = mod_ref[0, 4:5, :]
    x1 = x_ref[0] + gt1 * jnp.dot(merged, wo_ref[...], preferred_element_type=F32)
    x1_ref[0] = x1
    y = x1 * lax.rsqrt(jnp.mean(x1 * x1, axis=-1, keepdims=True) + EPS)
    h2_ref[0] = ((y * g2_ref[...]) * (1.0 + sc2) + sh2).astype(BF16)


def _mid(x, merged, mod3, g2, w_out):
    b, n, _ = x.shape
    tm = TM_MID
    tok = pl.BlockSpec((1, tm, D_MODEL), lambda bi, i: (bi, i, 0))
    return pl.pallas_call(
        _mid_kernel,
        grid=(b, n // tm),
        in_specs=[tok, tok,
                  pl.BlockSpec((1, 6, D_MODEL), lambda bi, i: (bi, 0, 0)),
                  _const_spec((1, D_MODEL)),
                  _const_spec((D_MODEL, D_MODEL))],
        out_specs=[tok, tok],
        out_shape=[jax.ShapeDtypeStruct((b, n, D_MODEL), F32),
                   jax.ShapeDtypeStruct((b, n, D_MODEL), BF16)],
        compiler_params=_cparams(("parallel", "parallel")),
        name="mid",
    )(x, merged, mod3, g2, w_out)


def _ffn_kernel(x1_ref, h_ref, hp_ref, hn_ref, mod_ref, wf_ref, cw_ref, cbias_ref, gf_ref, wd_ref,
                y_ref, gs_sc, us_sc, *, tm, nt):
    i = pl.program_id(1)
    rows = tm + 2 * HALO
    h2 = jnp.concatenate([hp_ref[0], h_ref[0], hn_ref[0]], axis=0)
    h2m = h_ref[0]
    ridx = lax.broadcasted_iota(jnp.int32, (rows, 1), 0)
    keep = jnp.logical_and(jnp.logical_or(i > 0, ridx >= HALO), jnp.logical_or(i < nt - 1, ridx < HALO + tm))
    n_chunks = D_FF // FF_CHUNK

    def ffn_in(c, slot):
        cs = slice(c * FF_CHUNK, (c + 1) * FF_CHUNK)
        gcs = slice(D_FF + c * FF_CHUNK, D_FF + (c + 1) * FF_CHUNK)
        us_sc[slot] = jnp.dot(h2m, wf_ref[:, cs], preferred_element_type=F32)
        gs_sc[slot] = jnp.where(keep, jnp.dot(h2, wf_ref[:, gcs], preferred_element_type=F32), 0.0)

    def ffn_out(c, slot, acc):
        cs = slice(c * FF_CHUNK, (c + 1) * FF_CHUNK)
        z = (cbias_ref[:, cs] + gs_sc[slot, HALO - 1:HALO - 1 + tm, :] * cw_ref[0:1, cs]
             + gs_sc[slot, HALO:HALO + tm, :] * cw_ref[1:2, cs]
             + gs_sc[slot, HALO + 1:HALO + 1 + tm, :] * cw_ref[2:3, cs])
        a = (z * (1.0 + lax.erf(z * math.sqrt(0.5))) * us_sc[slot]).astype(BF16)
        d = jnp.dot(a, wd_ref[cs, :], preferred_element_type=F32)
        return d if acc is None else acc + d

    ffn_in(0, 0)
    acc = None
    for c in range(n_chunks):
        if c + 1 < n_chunks:
            ffn_in(c + 1, (c + 1) % 2)
        acc = ffn_out(c, c % 2, acc)
    gt2 = mod_ref[0, 5:6, :]
    x2 = x1_ref[0] + gt2 * acc
    y_ref[0] = x2 * lax.rsqrt(jnp.mean(x2 * x2, axis=-1, keepdims=True) + EPS) * gf_ref[...]


def _ffn(x1, h2, mod3, w_ffn_in, conv_w, conv_b, g_final, w_down_half):
    b, n, _ = x1.shape
    tm = TM_FFN
    nt = n // tm
    rb = tm // HALO
    tok = pl.BlockSpec((1, tm, D_MODEL), lambda bi, i: (bi, i, 0))
    prv = pl.BlockSpec((1, HALO, D_MODEL), lambda bi, i: (bi, jnp.maximum(i * rb - 1, 0), 0))
    nxt = pl.BlockSpec((1, HALO, D_MODEL), lambda bi, i: (bi, jnp.minimum((i + 1) * rb, n // HALO - 1), 0))
    return pl.pallas_call(
        functools.partial(_ffn_kernel, tm=tm, nt=nt),
        grid=(b, nt),
        in_specs=[tok, tok, prv, nxt,
                  pl.BlockSpec((1, 6, D_MODEL), lambda bi, i: (bi, 0, 0)),
                  _const_spec((D_MODEL, 2 * D_FF)),
                  _const_spec((CONV_W, D_FF)), _const_spec((1, D_FF)),
                  _const_spec((1, D_MODEL)),
                  _const_spec((D_FF, D_MODEL))],
        out_specs=tok,
        out_shape=jax.ShapeDtypeStruct((b, n, D_MODEL), F32),
        scratch_shapes=[pltpu.VMEM((2, tm + 2 * HALO, FF_CHUNK), F32),
                        pltpu.VMEM((2, tm, FF_CHUNK), F32)],
        compiler_params=_cparams(("parallel", "parallel")),
        name="ffn",
    )(x1, h2, h2, h2, mod3, w_ffn_in, conv_w, conv_b, g_final, w_down_half)


def _rope_tables(n):
    t = jnp.arange(n, dtype=jnp.int32)
    row = (t // GRID_W).astype(F32)
    col = (t % GRID_W).astype(F32)
    inv = ROPE_THETA ** (-jnp.arange(0, ROPE_AXIS_DIM, 2, dtype=F32) / ROPE_AXIS_DIM)
    ar = row[:, None] * inv[None, :]
    ac = col[:, None] * inv[None, :]
    cos_t = jnp.concatenate([jnp.cos(ar), jnp.cos(ar), jnp.cos(ac), jnp.cos(ac)], axis=-1)
    sin_t = jnp.concatenate([-jnp.sin(ar), jnp.sin(ar), -jnp.sin(ac), jnp.sin(ac)], axis=-1)
    return cos_t, sin_t


def _trunk(x, mod, p):
    b, n, _ = x.shape
    mod3 = mod.reshape(b, 6, D_MODEL)
    cos_t, sin_t = p['rope']
    qa, ka, va, qb, kb, vb, ga, gb = _inproj(x, mod3, p['g_norm1'], p['g_qnorm'], p['g_knorm'],
                                             cos_t, sin_t, p['w_in'])
    gated_a = _gqa(qa, ka, va, ga)
    merged = _diff(qb, kb, vb, p['bias_t'], p['cb'], p['lam4'], p['g_subln'], gated_a, gb)
    x1, h2 = _mid(x, merged, mod3, p['g_norm2'], p['w_out'])
    return _ffn(x1, h2, mod3, p['w_ffn_in'], p['conv_w'], p['conv_b'], p['g_final'], p['w_down'])


def kernel(x_prompt, x_sample, c_prompt, c_sample, w_mod, b_mod, g_norm1, w_in, g_qnorm, g_knorm,
           lambda_q1, lambda_k1, lambda_q2, lambda_k2, g_subln, rel_bias, w_out, g_norm2,
           w_ffn_in, conv_w, conv_b, w_down, g_final):
    bp, bs = x_prompt.shape[0], x_sample.shape[0]
    rows = bp + bs
    pad = (-rows) % 8
    c_all = jnp.pad(jnp.concatenate([c_prompt, c_sample], axis=0), ((0, pad), (0, 0)))
    mod = _mod(c_all, w_mod[0], b_mod)

    p = {
        'g_norm1': g_norm1, 'g_qnorm': g_qnorm, 'g_knorm': g_knorm, 'g_subln': g_subln,
        'g_norm2': g_norm2, 'g_final': g_final.reshape(1, D_MODEL),
        'w_in': w_in[0].astype(BF16), 'w_out': w_out[0].astype(BF16),
        'w_ffn_in': w_ffn_in[0].astype(BF16), 'w_down': (w_down[0] * 0.5).astype(BF16),
        'conv_w': conv_w[0], 'conv_b': conv_b,
        'lam4': jnp.concatenate([lambda_q1, lambda_k1, lambda_q2, lambda_k2], axis=0),
        'bias_t': _bias_tiles(rel_bias, T_B),
        'rope': _rope_tables(max(x_prompt.shape[1], x_sample.shape[1])),
        'cb': jnp.stack([rel_bias[N_BUCKETS // 2 - 1], rel_bias[N_BUCKETS - 1]], axis=1) * LOG2E,
    }
    y_prompt = _trunk(x_prompt, mod[:bp], p)
    y_sample = _trunk(x_sample, mod[bp:rows], p)
    return (y_prompt, y_sample)
```

```python
import functools
import math

import jax
import jax.numpy as jnp
from jax import lax
from jax.experimental import pallas as pl
from jax.experimental.pallas import tpu as pltpu

F32 = jnp.float32
BF16 = jnp.bfloat16

D_MODEL = 1024
GRID_W = 64
EPS = 1e-6
HA_Q, HA_KV, G_A, HD_A = 8, 2, 4, 128
ROPE_AXIS_DIM = HD_A // 2
ROPE_THETA = 10000.0
HB, DH_B, DV_B = 8, 64, 128
N_BUCKETS, MAX_DIST = 32, 128
D_FF = 2816
CONV_W = 3
LAMBDA_INIT = 0.8 - 0.6 * math.exp(-0.3 * 0)
LOG2E = 1.4426950408889634
SCALE_A = HD_A ** -0.5 * LOG2E
SCALE_B = DH_B ** -0.5 * LOG2E
NEG_BIG = -0.7 * float(jnp.finfo(jnp.float32).max)

OFF_QA, OFF_KA, OFF_VA = 0, 1024, 1280
OFF_QB, OFF_KB, OFF_VB = 1536, 2560, 3584
OFF_GA, OFF_GB = 4608, 5632
N_IN = 6656

LANES = 128
VMEM_LIMIT = 56 * 1024 * 1024

TM_IN = 512
PROJ_CHUNK = 512
TM_MID = 512
TM_FFN = 512
HALO = 16
FF_CHUNK = 256
TQ_A = 1024
TK_A = 1024
MOD_BLK = 1024
ROPE_SWAP = ROPE_AXIS_DIM // 2
T_B = 512


def _cparams(sem):
    return pltpu.CompilerParams(dimension_semantics=sem, vmem_limit_bytes=VMEM_LIMIT)


def _const_spec(shape):
    nd = len(shape)
    return pl.BlockSpec(shape, lambda *_: (0,) * nd, pipeline_mode=pl.Buffered(1))


def _mod_kernel(c_ref, w_ref, b_ref, o_ref):
    c = c_ref[...]
    sc = c * jax.nn.sigmoid(c)
    o_ref[...] = jnp.dot(sc, w_ref[...], preferred_element_type=F32,
                         precision=lax.Precision.HIGHEST) + b_ref[...]


def _mod(c, w_mod, b_mod):
    rows = c.shape[0]
    n_out = w_mod.shape[1]
    blk = MOD_BLK
    return pl.pallas_call(
        _mod_kernel,
        grid=(n_out // blk,),
        in_specs=[pl.BlockSpec((rows, D_MODEL), lambda j: (0, 0)),
                  pl.BlockSpec((D_MODEL, blk), lambda j: (0, j)),
                  pl.BlockSpec((1, blk), lambda j: (0, j))],
        out_specs=pl.BlockSpec((rows, blk), lambda j: (0, j)),
        out_shape=jax.ShapeDtypeStruct((rows, n_out), F32),
        compiler_params=_cparams(("parallel",)),
        name="mod",
    )(c, w_mod, b_mod)


def _bias_kernel(tab_ref, o_ref, *, t):
    cls = pl.program_id(0)
    row = lax.broadcasted_iota(jnp.int32, (t, t), 0)
    col = lax.broadcasted_iota(jnp.int32, (t, t), 1)
    rel = (cls - 1) * t + col - row
    nb = N_BUCKETS // 2
    max_exact = nb // 2
    ret = jnp.where(rel > 0, nb, 0)
    n = jnp.abs(rel)
    large = max_exact + (jnp.log(jnp.maximum(n, 1).astype(F32) / max_exact)
                         / math.log(MAX_DIST / max_exact) * (nb - max_exact)).astype(jnp.int32)
    large = jnp.minimum(large, nb - 1)
    bucket = ret + jnp.where(n < max_exact, n, large)
    for h in range(HB):
        acc = jnp.zeros((t, t), F32)
        for b in range(N_BUCKETS):
            acc = jnp.where(bucket == b, tab_ref[b, h], acc)
        o_ref[h, 0] = acc * LOG2E


def _bias_tiles(rel_bias, t):
    return pl.pallas_call(
        functools.partial(_bias_kernel, t=t),
        grid=(3,),
        in_specs=[pl.BlockSpec(memory_space=pltpu.SMEM)],
        out_specs=pl.BlockSpec((HB, 1, t, t), lambda c: (0, c, 0, 0)),
        out_shape=jax.ShapeDtypeStruct((HB, 3, t, t), F32),
        compiler_params=_cparams(("parallel",)),
        name="bias_tiles",
    )(rel_bias)


def _inproj_kernel(x_ref, mod_ref, g1_ref, gq_ref, gk_ref, cos_ref, sin_ref, w_ref,
                   qa_ref, ka_ref, va_ref, qb_ref, kb_ref, vb_ref, ga_ref, gb_ref, *, tm):
    x = x_ref[0]
    y = x * lax.rsqrt(jnp.mean(x * x, axis=-1, keepdims=True) + EPS)
    sh1 = mod_ref[0, 0:1, :]
    sc1 = mod_ref[0, 1:2, :]
    hb = ((y * g1_ref[...]) * (1.0 + sc1) + sh1).astype(BF16)

    cos = cos_ref[...]
    sin = sin_ref[...]
    lane = lax.broadcasted_iota(jnp.int32, (tm, LANES), 1)
    low_half = (lane & ROPE_SWAP) == 0

    def proj(c0, width):
        return jnp.dot(hb, w_ref[:, c0:c0 + width], preferred_element_type=F32)

    def norm_rope(p, g, scale):
        yn = p * lax.rsqrt(jnp.mean(p * p, axis=-1, keepdims=True) + EPS) * g
        partner = jnp.where(low_half, pltpu.roll(yn, LANES - ROPE_SWAP, 1), pltpu.roll(yn, ROPE_SWAP, 1))
        r = yn * cos + partner * sin
        return r * scale if scale is not None else r

    gq = gq_ref[...]
    gk = gk_ref[...]
    for c in range(2):
        ga_ref[0, :, c * PROJ_CHUNK:(c + 1) * PROJ_CHUNK] = jax.nn.sigmoid(proj(OFF_GA + c * PROJ_CHUNK, PROJ_CHUNK)).astype(BF16)
    for c in range(2):
        gb_ref[0, :, c * PROJ_CHUNK:(c + 1) * PROJ_CHUNK] = jax.nn.sigmoid(proj(OFF_GB + c * PROJ_CHUNK, PROJ_CHUNK)).astype(BF16)
    for c in range(2):
        p = proj(OFF_QA + c * PROJ_CHUNK, PROJ_CHUNK)
        for j in range(4):
            qa_ref[0, c * 4 + j] = norm_rope(p[:, j * LANES:(j + 1) * LANES], gq, SCALE_A).astype(BF16)
    p = proj(OFF_KA, PROJ_CHUNK)
    for j in range(2):
        ka_ref[0, j] = norm_rope(p[:, j * LANES:(j + 1) * LANES], gk, None).astype(BF16)
        va_ref[0, j] = p[:, (2 + j) * LANES:(3 + j) * LANES].astype(BF16)
    for c in range(2):
        p = proj(OFF_QB + c * PROJ_CHUNK, PROJ_CHUNK)
        for j in range(4):
            qb_ref[0, c * 4 + j] = (p[:, j * LANES:(j + 1) * LANES] * SCALE_B).astype(BF16)
    for c in range(2):
        p = proj(OFF_KB + c * PROJ_CHUNK, PROJ_CHUNK)
        for j in range(4):
            kb_ref[0, c * 4 + j] = p[:, j * LANES:(j + 1) * LANES].astype(BF16)
    for c in range(2):
        p = proj(OFF_VB + c * PROJ_CHUNK, PROJ_CHUNK)
        for j in range(4):
            vb_ref[0, c * 4 + j] = p[:, j * LANES:(j + 1) * LANES].astype(BF16)


def _inproj(x, mod3, g1, gq, gk, cos_t, sin_t, w_in):
    b, n, _ = x.shape
    tm = TM_IN
    hm = lambda heads: pl.BlockSpec((1, heads, tm, LANES), lambda bi, i: (bi, 0, i, 0))
    tok = pl.BlockSpec((1, tm, D_MODEL), lambda bi, i: (bi, i, 0))
    hshape = lambda heads: jax.ShapeDtypeStruct((b, heads, n, LANES), BF16)
    return pl.pallas_call(
        functools.partial(_inproj_kernel, tm=tm),
        grid=(b, n // tm),
        in_specs=[tok,
                  pl.BlockSpec((1, 6, D_MODEL), lambda bi, i: (bi, 0, 0)),
                  _const_spec((1, D_MODEL)), _const_spec((1, LANES)), _const_spec((1, LANES)),
                  pl.BlockSpec((tm, LANES), lambda bi, i: (i, 0)),
                  pl.BlockSpec((tm, LANES), lambda bi, i: (i, 0)),
                  _const_spec((D_MODEL, N_IN))],
        out_specs=[hm(HA_Q), hm(HA_KV), hm(HA_KV), hm(HB), hm(HB), hm(HB), tok, tok],
        out_shape=[hshape(HA_Q), hshape(HA_KV), hshape(HA_KV), hshape(HB), hshape(HB), hshape(HB),
                   jax.ShapeDtypeStruct((b, n, D_MODEL), BF16), jax.ShapeDtypeStruct((b, n, D_MODEL), BF16)],
        compiler_params=_cparams(("parallel", "parallel")),
        name="inproj",
    )(x, mod3, g1, gq, gk, cos_t, sin_t, w_in)


def _stage_scores(s, s_slot, mc_slot):
    s_slot[...] = s
    mc_slot[...] = jnp.broadcast_to(jnp.max(s, axis=1, keepdims=True), mc_slot.shape)


def _stage_update(s_slot, mc_slot, shift, m_ref, accl_ref, v, tk):
    m_prev = m_ref[...]
    if shift is None:
        m_next = jnp.maximum(m_prev, mc_slot[...])
        sub = m_next
    else:
        m_next = jnp.maximum(m_prev, mc_slot[...] + shift)
        sub = m_next - shift
    p = jnp.exp2(s_slot[...] - jnp.concatenate([sub] * (tk // LANES), axis=1)).astype(BF16)
    alpha = jnp.exp2(m_prev - m_next)
    v_ones = jnp.concatenate([v, jnp.ones((tk, LANES), BF16)], axis=1)
    m_ref[...] = m_next
    accl_ref[...] = (accl_ref[...] * jnp.concatenate([alpha, alpha], axis=1)
                     + jnp.dot(p, v_ones, preferred_element_type=F32))


def _pipelined(n_units, scores_fn, update_fn):
    scores_fn(0, 0)
    for u in range(n_units):
        if u + 1 < n_units:
            scores_fn(u + 1, (u + 1) % 2)
        update_fn(u, u % 2)


def _gqa_kernel(q_ref, k_ref, v_ref, sa_ref, o_ref, m_sc, accl_sc, s_sc, mc_sc, *, tq, ts, n_sub, nk):
    ki = pl.program_id(2)

    @pl.when(ki == 0)
    def _():
        m_sc[...] = jnp.full(m_sc.shape, NEG_BIG, F32)
        accl_sc[...] = jnp.zeros(accl_sc.shape, F32)

    def scores(u, slot):
        sub, h = divmod(u, HA_Q)
        k = k_ref[0, h // G_A, sub * ts:(sub + 1) * ts]
        s = lax.dot_general(q_ref[0, h], k, (((1,), (1,)), ((), ())), preferred_element_type=F32)
        _stage_scores(s, s_sc.at[slot], mc_sc.at[slot])

    def update(u, slot):
        sub, h = divmod(u, HA_Q)
        _stage_update(s_sc.at[slot], mc_sc.at[slot], None, m_sc.at[h], accl_sc.at[h],
                      v_ref[0, h // G_A, sub * ts:(sub + 1) * ts], ts)

    _pipelined(n_sub * HA_Q, scores, update)

    @pl.when(ki == nk - 1)
    def _():
        for h in range(HA_Q):
            cols = slice(h * HD_A, (h + 1) * HD_A)
            o = accl_sc[h, :, :HD_A] / accl_sc[h, :, HD_A:]
            o_ref[0, :, cols] = (sa_ref[0, :, cols].astype(F32) * o).astype(BF16)


def _gqa(qa, ka, va, sa):
    b, _, n, _ = qa.shape
    tq, ts, n_sub = TQ_A, TK_A, 1
    tk = ts * n_sub
    nq, nk = n // tq, n // tk
    return pl.pallas_call(
        functools.partial(_gqa_kernel, tq=tq, ts=ts, n_sub=n_sub, nk=nk),
        grid=(b, nq, nk),
        in_specs=[pl.BlockSpec((1, HA_Q, tq, HD_A), lambda bi, qi, ki: (bi, 0, qi, 0)),
                  pl.BlockSpec((1, HA_KV, tk, HD_A), lambda bi, qi, ki: (bi, 0, ki, 0)),
                  pl.BlockSpec((1, HA_KV, tk, HD_A), lambda bi, qi, ki: (bi, 0, ki, 0)),
                  pl.BlockSpec((1, tq, D_MODEL), lambda bi, qi, ki: (bi, qi, 0))],
        out_specs=pl.BlockSpec((1, tq, D_MODEL), lambda bi, qi, ki: (bi, qi, 0)),
        out_shape=jax.ShapeDtypeStruct((b, n, D_MODEL), BF16),
        scratch_shapes=[pltpu.VMEM((HA_Q, tq, LANES), F32),
                        pltpu.VMEM((HA_Q, tq, HD_A + LANES), F32),
                        pltpu.VMEM((2, tq, ts), F32),
                        pltpu.VMEM((2, tq, LANES), F32)],
        compiler_params=_cparams(("parallel", "parallel", "arbitrary")),
        name="gqa",
    )(qa, ka, va, sa)


def _diff_kernel(cb_ref, lam_ref, gs_ref, q_ref, k_ref, v_ref, bias_ref, ga_ref, sb_ref, o_ref,
                 q2_sc, m_sc, accl_sc, s_sc, mc_sc, *, t, nk):
    qi = pl.program_id(1)
    ki = pl.program_id(2)

    @pl.when(ki == 0)
    def _():
        m_sc[...] = jnp.full(m_sc.shape, NEG_BIG, F32)
        accl_sc[...] = jnp.zeros(accl_sc.shape, F32)
        first = lax.broadcasted_iota(jnp.int32, (t, LANES), 1) < DH_B
        zero = jnp.zeros((t, LANES), BF16)
        for h in range(HB):
            q = q_ref[0, h]
            q2_sc[h, :t] = jnp.where(first, q, zero)
            q2_sc[h, t:] = jnp.where(first, zero, q)

    def heads(near):
        def scores(u, slot):
            h, half = divmod(u, 2)
            s = lax.dot_general(q2_sc[h, half * t:(half + 1) * t], k_ref[0, h], (((1,), (1,)), ((), ())),
                                preferred_element_type=F32)
            if near:
                s = s + bias_ref[h, 0]
            _stage_scores(s, s_sc.at[slot], mc_sc.at[slot])

        def update(u, slot):
            h, half = divmod(u, 2)
            rs = slice(half * t, (half + 1) * t)
            shift = None if near else jnp.where(ki > qi, cb_ref[h, 1], cb_ref[h, 0])
            _stage_update(s_sc.at[slot], mc_sc.at[slot], shift, m_sc.at[h, rs], accl_sc.at[h, rs], v_ref[0, h], t)

        _pipelined(2 * HB, scores, update)

    near_diag = jnp.abs(ki - qi) <= 1

    @pl.when(near_diag)
    def _():
        heads(True)

    @pl.when(jnp.logical_not(near_diag))
    def _():
        heads(False)

    @pl.when(ki == nk - 1)
    def _():
        lam = (jnp.exp(jnp.sum(lam_ref[0:1, :] * lam_ref[1:2, :], axis=-1, keepdims=True))
               - jnp.exp(jnp.sum(lam_ref[2:3, :] * lam_ref[3:4, :], axis=-1, keepdims=True))
               + LAMBDA_INIT)
        gs = gs_ref[...]
        for h in range(HB):
            o12 = accl_sc[h, :, :DV_B] / accl_sc[h, :, DV_B:]
            o = o12[:t] - lam * o12[t:]
            o = o * lax.rsqrt(jnp.mean(o * o, axis=-1, keepdims=True) + EPS) * gs * (1.0 - LAMBDA_INIT)
            cols = slice(h * DV_B, (h + 1) * DV_B)
            merged = ga_ref[0, :, cols].astype(F32) + sb_ref[0, :, cols].astype(F32) * o
            o_ref[0, :, cols] = merged.astype(BF16)


def _diff(qb, kb, vb, bias_t, cb, lam4, gs, gated_a, sb):
    b, _, n, _ = qb.shape
    t = T_B
    nq = nk = n // t
    hm = lambda idx: pl.BlockSpec((1, HB, t, LANES), idx)
    return pl.pallas_call(
        functools.partial(_diff_kernel, t=t, nk=nk),
        grid=(b, nq, nk),
        in_specs=[pl.BlockSpec(memory_space=pltpu.SMEM),
                  _const_spec((4, DH_B)), _const_spec((1, DV_B)),
                  hm(lambda bi, qi, ki: (bi, 0, qi, 0)),
                  hm(lambda bi, qi, ki: (bi, 0, ki, 0)),
                  hm(lambda bi, qi, ki: (bi, 0, ki, 0)),
                  pl.BlockSpec((HB, 1, t, t), lambda bi, qi, ki: (0, jnp.clip(ki - qi, -1, 1) + 1, 0, 0)),
                  pl.BlockSpec((1, t, D_MODEL), lambda bi, qi, ki: (bi, qi, 0)),
                  pl.BlockSpec((1, t, D_MODEL), lambda bi, qi, ki: (bi, qi, 0))],
        out_specs=pl.BlockSpec((1, t, D_MODEL), lambda bi, qi, ki: (bi, qi, 0)),
        out_shape=jax.ShapeDtypeStruct((b, n, D_MODEL), BF16),
        scratch_shapes=[pltpu.VMEM((HB, 2 * t, LANES), BF16),
                        pltpu.VMEM((HB, 2 * t, LANES), F32),
                        pltpu.VMEM((HB, 2 * t, DV_B + LANES), F32),
                        pltpu.VMEM((2, t, t), F32),
                        pltpu.VMEM((2, t, LANES), F32)],
        compiler_params=_cparams(("parallel", "parallel", "arbitrary")),
        name="diff",
    )(cb, lam4, gs, qb, kb, vb, bias_t, gated_a, sb)


def _mid_kernel(x_ref, merged_ref, mod_ref, g2_ref, wo_ref, x1_ref, h2_ref):
    merged = merged_ref[0]
    gt1 = mod_ref[0, 2:3, :]
    sh2 = mod_ref[0, 3:4, :]
    sc2 = mod_ref[0, 4:5, :]
    x1 = x_ref[0] + gt1 * jnp.dot(merged, wo_ref[...], preferred_element_type=F32)
    x1_ref[0] = x1
    y = x1 * lax.rsqrt(jnp.mean(x1 * x1, axis=-1, keepdims=True) + EPS)
    h2_ref[0] = ((y * g2_ref[...]) * (1.0 + sc2) + sh2).astype(BF16)


def _mid(x, merged, mod3, g2, w_out):
    b, n, _ = x.shape
    tm = TM_MID
    tok = pl.BlockSpec((1, tm, D_MODEL), lambda bi, i: (bi, i, 0))
    return pl.pallas_call(
        _mid_kernel,
        grid=(b, n // tm),
        in_specs=[tok, tok,
                  pl.BlockSpec((1, 6, D_MODEL), lambda bi, i: (bi, 0, 0)),
                  _const_spec((1, D_MODEL)),
                  _const_spec((D_MODEL, D_MODEL))],
        out_specs=[tok, tok],
        out_shape=[jax.ShapeDtypeStruct((b, n, D_MODEL), F32),
                   jax.ShapeDtypeStruct((b, n, D_MODEL), BF16)],
        compiler_params=_cparams(("parallel", "parallel")),
        name="mid",
    )(x, merged, mod3, g2, w_out)


def _ffn_kernel(x1_ref, h_ref, hp_ref, hn_ref, mod_ref, wf_ref, cw_ref, cbias_ref, gf_ref, wd_ref,
                y_ref, gs_sc, us_sc, *, tm, nt):
    i = pl.program_id(1)
    rows = tm + 2 * HALO
    h2 = jnp.concatenate([hp_ref[0], h_ref[0], hn_ref[0]], axis=0)
    h2m = h_ref[0]
    ridx = lax.broadcasted_iota(jnp.int32, (rows, 1), 0)
    keep = jnp.logical_and(jnp.logical_or(i > 0, ridx >= HALO), jnp.logical_or(i < nt - 1, ridx < HALO + tm))
    n_chunks = D_FF // FF_CHUNK

    def ffn_in(c, slot):
        cs = slice(c * FF_CHUNK, (c + 1) * FF_CHUNK)
        gcs = slice(D_FF + c * FF_CHUNK, D_FF + (c + 1) * FF_CHUNK)
        us_sc[slot] = jnp.dot(h2m, wf_ref[:, cs], preferred_element_type=F32)
        gs_sc[slot] = jnp.where(keep, jnp.dot(h2, wf_ref[:, gcs], preferred_element_type=F32), 0.0)

    def ffn_out(c, slot, acc):
        cs = slice(c * FF_CHUNK, (c + 1) * FF_CHUNK)
        z = (cbias_ref[:, cs] + gs_sc[slot, HALO - 1:HALO - 1 + tm, :] * cw_ref[0:1, cs]
             + gs_sc[slot, HALO:HALO + tm, :] * cw_ref[1:2, cs]
             + gs_sc[slot, HALO + 1:HALO + 1 + tm, :] * cw_ref[2:3, cs])
        a = (z * (1.0 + lax.erf(z * math.sqrt(0.5))) * us_sc[slot]).astype(BF16)
        d = jnp.dot(a, wd_ref[cs, :], preferred_element_type=F32)
        return d if acc is None else acc + d

    ffn_in(0, 0)
    acc = None
    for c in range(n_chunks):
        if c + 1 < n_chunks:
            ffn_in(c + 1, (c + 1) % 2)
        acc = ffn_out(c, c % 2, acc)
    gt2 = mod_ref[0, 5:6, :]
    x2 = x1_ref[0] + gt2 * acc
    y_ref[0] = x2 * lax.rsqrt(jnp.mean(x2 * x2, axis=-1, keepdims=True) + EPS) * gf_ref[...]


def _ffn(x1, h2, mod3, w_ffn_in, conv_w, conv_b, g_final, w_down_half):
    b, n, _ = x1.shape
    tm = TM_FFN
    nt = n // tm
    rb = tm // HALO
    tok = pl.BlockSpec((1, tm, D_MODEL), lambda bi, i: (bi, i, 0))
    prv = pl.BlockSpec((1, HALO, D_MODEL), lambda bi, i: (bi, jnp.maximum(i * rb - 1, 0), 0))
    nxt = pl.BlockSpec((1, HALO, D_MODEL), lambda bi, i: (bi, jnp.minimum((i + 1) * rb, n // HALO - 1), 0))
    return pl.pallas_call(
        functools.partial(_ffn_kernel, tm=tm, nt=nt),
        grid=(b, nt),
        in_specs=[tok, tok, prv, nxt,
                  pl.BlockSpec((1, 6, D_MODEL), lambda bi, i: (bi, 0, 0)),
                  _const_spec((D_MODEL, 2 * D_FF)),
                  _const_spec((CONV_W, D_FF)), _const_spec((1, D_FF)),
                  _const_spec((1, D_MODEL)),
                  _const_spec((D_FF, D_MODEL))],
        out_specs=tok,
        out_shape=jax.ShapeDtypeStruct((b, n, D_MODEL), F32),
        scratch_shapes=[pltpu.VMEM((2, tm + 2 * HALO, FF_CHUNK), F32),
                        pltpu.VMEM((2, tm, FF_CHUNK), F32)],
        compiler_params=_cparams(("parallel", "parallel")),
        name="ffn",
    )(x1, h2, h2, h2, mod3, w_ffn_in, conv_w, conv_b, g_final, w_down_half)


def _rope_tables(n):
    t = jnp.arange(n, dtype=jnp.int32)
    row = (t // GRID_W).astype(F32)
    col = (t % GRID_W).astype(F32)
    inv = ROPE_THETA ** (-jnp.arange(0, ROPE_AXIS_DIM, 2, dtype=F32) / ROPE_AXIS_DIM)
    ar = row[:, None] * inv[None, :]
    ac = col[:, None] * inv[None, :]
    cos_t = jnp.concatenate([jnp.cos(ar), jnp.cos(ar), jnp.cos(ac), jnp.cos(ac)], axis=-1)
    sin_t = jnp.concatenate([-jnp.sin(ar), jnp.sin(ar), -jnp.sin(ac), jnp.sin(ac)], axis=-1)
    return cos_t, sin_t


def _trunk(x, mod, p):
    b, n, _ = x.shape
    mod3 = mod.reshape(b, 6, D_MODEL)
    cos_t, sin_t = p['rope']
    qa, ka, va, qb, kb, vb, ga, gb = _inproj(x, mod3, p['g_norm1'], p['g_qnorm'], p['g_knorm'],
                                             cos_t, sin_t, p['w_in'])
    gated_a = _gqa(qa, ka, va, ga)
    merged = _diff(qb, kb, vb, p['bias_t'], p['cb'], p['lam4'], p['g_subln'], gated_a, gb)
    x1, h2 = _mid(x, merged, mod3, p['g_norm2'], p['w_out'])
    return _ffn(x1, h2, mod3, p['w_ffn_in'], p['conv_w'], p['conv_b'], p['g_final'], p['w_down'])


def kernel(x_prompt, x_sample, c_prompt, c_sample, w_mod, b_mod, g_norm1, w_in, g_qnorm, g_knorm,
           lambda_q1, lambda_k1, lambda_q2, lambda_k2, g_subln, rel_bias, w_out, g_norm2,
           w_ffn_in, conv_w, conv_b, w_down, g_final):
    bp, bs = x_prompt.shape[0], x_sample.shape[0]
    rows = bp + bs
    pad = (-rows) % 8
    c_all = jnp.pad(jnp.concatenate([c_prompt, c_sample], axis=0), ((0, pad), (0, 0)))
    mod = _mod(c_all, w_mod[0], b_mod)

    p = {
        'g_norm1': g_norm1, 'g_qnorm': g_qnorm, 'g_knorm': g_knorm, 'g_subln': g_subln,
        'g_norm2': g_norm2, 'g_final': g_final.reshape(1, D_MODEL),
        'w_in': w_in[0].astype(BF16), 'w_out': w_out[0].astype(BF16),
        'w_ffn_in': w_ffn_in[0].astype(BF16), 'w_down': (w_down[0] * 0.5).astype(BF16),
        'conv_w': conv_w[0], 'conv_b': conv_b,
        'lam4': jnp.concatenate([lambda_q1, lambda_k1, lambda_q2, lambda_k2], axis=0),
        'bias_t': _bias_tiles(rel_bias, T_B),
        'rope': _rope_tables(max(x_prompt.shape[1], x_sample.shape[1])),
        'cb': jnp.stack([rel_bias[N_BUCKETS // 2 - 1], rel_bias[N_BUCKETS - 1]], axis=1) * LOG2E,
    }
    y_prompt = _trunk(x_prompt, mod[:bp], p)
    y_sample = _trunk(x_sample, mod[bp:rows], p)
    return (y_prompt, y_sample)
```

```python
import functools
import math

import jax
import jax.numpy as jnp
from jax import lax
from jax.experimental import pallas as pl
from jax.experimental.pallas import tpu as pltpu

F32 = jnp.float32
BF16 = jnp.bfloat16

D_MODEL = 1024
GRID_W = 64
EPS = 1e-6
HA_Q, HA_KV, G_A, HD_A = 8, 2, 4, 128
ROPE_AXIS_DIM = HD_A // 2
ROPE_THETA = 10000.0
HB, DH_B, DV_B = 8, 64, 128
N_BUCKETS, MAX_DIST = 32, 128
D_FF = 2816
CONV_W = 3
LAMBDA_INIT = 0.8 - 0.6 * math.exp(-0.3 * 0)
LOG2E = 1.4426950408889634
SCALE_A = HD_A ** -0.5 * LOG2E
SCALE_B = DH_B ** -0.5 * LOG2E
NEG_BIG = -0.7 * float(jnp.finfo(jnp.float32).max)

OFF_QA, OFF_KA, OFF_VA = 0, 1024, 1280
OFF_QB, OFF_KB, OFF_VB = 1536, 2560, 3584
OFF_GA, OFF_GB = 4608, 5632
N_IN = 6656

LANES = 128
VMEM_LIMIT = 56 * 1024 * 1024

TM_IN = 512
PROJ_CHUNK = 512
TM_MID = 512
TM_FFN = 512
HALO = 16
FF_CHUNK = 256
TQ_A = 1024
TK_A = 2048
MOD_BLK = 1024
ROPE_SWAP = ROPE_AXIS_DIM // 2
T_B = 512


def _cparams(sem):
    return pltpu.CompilerParams(dimension_semantics=sem, vmem_limit_bytes=VMEM_LIMIT)


def _const_spec(shape):
    nd = len(shape)
    return pl.BlockSpec(shape, lambda *_: (0,) * nd, pipeline_mode=pl.Buffered(1))


def _mod_kernel(c_ref, w_ref, b_ref, o_ref):
    c = c_ref[...]
    sc = c * jax.nn.sigmoid(c)
    o_ref[...] = jnp.dot(sc, w_ref[...], preferred_element_type=F32,
                         precision=lax.Precision.HIGHEST) + b_ref[...]


def _mod(c, w_mod, b_mod):
    rows = c.shape[0]
    n_out = w_mod.shape[1]
    blk = MOD_BLK
    return pl.pallas_call(
        _mod_kernel,
        grid=(n_out // blk,),
        in_specs=[pl.BlockSpec((rows, D_MODEL), lambda j: (0, 0)),
                  pl.BlockSpec((D_MODEL, blk), lambda j: (0, j)),
                  pl.BlockSpec((1, blk), lambda j: (0, j))],
        out_specs=pl.BlockSpec((rows, blk), lambda j: (0, j)),
        out_shape=jax.ShapeDtypeStruct((rows, n_out), F32),
        compiler_params=_cparams(("parallel",)),
        name="mod",
    )(c, w_mod, b_mod)


def _bias_kernel(tab_ref, o_ref, *, t):
    cls = pl.program_id(0)
    row = lax.broadcasted_iota(jnp.int32, (t, t), 0)
    col = lax.broadcasted_iota(jnp.int32, (t, t), 1)
    rel = (cls - 1) * t + col - row
    nb = N_BUCKETS // 2
    max_exact = nb // 2
    ret = jnp.where(rel > 0, nb, 0)
    n = jnp.abs(rel)
    large = max_exact + (jnp.log(jnp.maximum(n, 1).astype(F32) / max_exact)
                         / math.log(MAX_DIST / max_exact) * (nb - max_exact)).astype(jnp.int32)
    large = jnp.minimum(large, nb - 1)
    bucket = ret + jnp.where(n < max_exact, n, large)
    for h in range(HB):
        acc = jnp.zeros((t, t), F32)
        for b in range(N_BUCKETS):
            acc = jnp.where(bucket == b, tab_ref[b, h], acc)
        o_ref[h, 0] = acc * LOG2E


def _bias_tiles(rel_bias, t):
    return pl.pallas_call(
        functools.partial(_bias_kernel, t=t),
        grid=(3,),
        in_specs=[pl.BlockSpec(memory_space=pltpu.SMEM)],
        out_specs=pl.BlockSpec((HB, 1, t, t), lambda c: (0, c, 0, 0)),
        out_shape=jax.ShapeDtypeStruct((HB, 3, t, t), F32),
        compiler_params=_cparams(("parallel",)),
        name="bias_tiles",
    )(rel_bias)


def _inproj_kernel(x_ref, mod_ref, g1_ref, gq_ref, gk_ref, cos_ref, sin_ref, w_ref,
                   qa_ref, ka_ref, va_ref, qb_ref, kb_ref, vb_ref, ga_ref, gb_ref, *, tm):
    x = x_ref[0]
    y = x * lax.rsqrt(jnp.mean(x * x, axis=-1, keepdims=True) + EPS)
    sh1 = mod_ref[0, 0:1, :]
    sc1 = mod_ref[0, 1:2, :]
    hb = ((y * g1_ref[...]) * (1.0 + sc1) + sh1).astype(BF16)

    cos = cos_ref[...]
    sin = sin_ref[...]
    lane = lax.broadcasted_iota(jnp.int32, (tm, LANES), 1)
    low_half = (lane & ROPE_SWAP) == 0

    def proj(c0, width):
        return jnp.dot(hb, w_ref[:, c0:c0 + width], preferred_element_type=F32)

    def norm_rope(p, g, scale):
        yn = p * lax.rsqrt(jnp.mean(p * p, axis=-1, keepdims=True) + EPS) * g
        partner = jnp.where(low_half, pltpu.roll(yn, LANES - ROPE_SWAP, 1), pltpu.roll(yn, ROPE_SWAP, 1))
        r = yn * cos + partner * sin
        return r * scale if scale is not None else r

    gq = gq_ref[...]
    gk = gk_ref[...]
    for c in range(2):
        ga_ref[0, :, c * PROJ_CHUNK:(c + 1) * PROJ_CHUNK] = jax.nn.sigmoid(proj(OFF_GA + c * PROJ_CHUNK, PROJ_CHUNK)).astype(BF16)
    for c in range(2):
        gb_ref[0, :, c * PROJ_CHUNK:(c + 1) * PROJ_CHUNK] = jax.nn.sigmoid(proj(OFF_GB + c * PROJ_CHUNK, PROJ_CHUNK)).astype(BF16)
    for c in range(2):
        p = proj(OFF_QA + c * PROJ_CHUNK, PROJ_CHUNK)
        for j in range(4):
            qa_ref[0, c * 4 + j] = norm_rope(p[:, j * LANES:(j + 1) * LANES], gq, SCALE_A).astype(BF16)
    p = proj(OFF_KA, PROJ_CHUNK)
    for j in range(2):
        ka_ref[0, j] = norm_rope(p[:, j * LANES:(j + 1) * LANES], gk, None).astype(BF16)
        va_ref[0, j] = p[:, (2 + j) * LANES:(3 + j) * LANES].astype(BF16)
    for c in range(2):
        p = proj(OFF_QB + c * PROJ_CHUNK, PROJ_CHUNK)
        for j in range(4):
            qb_ref[0, c * 4 + j] = (p[:, j * LANES:(j + 1) * LANES] * SCALE_B).astype(BF16)
    for c in range(2):
        p = proj(OFF_KB + c * PROJ_CHUNK, PROJ_CHUNK)
        for j in range(4):
            kb_ref[0, c * 4 + j] = p[:, j * LANES:(j + 1) * LANES].astype(BF16)
    for c in range(2):
        p = proj(OFF_VB + c * PROJ_CHUNK, PROJ_CHUNK)
        for j in range(4):
            vb_ref[0, c * 4 + j] = p[:, j * LANES:(j + 1) * LANES].astype(BF16)


def _inproj(x, mod3, g1, gq, gk, cos_t, sin_t, w_in):
    b, n, _ = x.shape
    tm = TM_IN
    hm = lambda heads: pl.BlockSpec((1, heads, tm, LANES), lambda bi, i: (bi, 0, i, 0))
    tok = pl.BlockSpec((1, tm, D_MODEL), lambda bi, i: (bi, i, 0))
    hshape = lambda heads: jax.ShapeDtypeStruct((b, heads, n, LANES), BF16)
    return pl.pallas_call(
        functools.partial(_inproj_kernel, tm=tm),
        grid=(b, n // tm),
        in_specs=[tok,
                  pl.BlockSpec((1, 6, D_MODEL), lambda bi, i: (bi, 0, 0)),
                  _const_spec((1, D_MODEL)), _const_spec((1, LANES)), _const_spec((1, LANES)),
                  pl.BlockSpec((tm, LANES), lambda bi, i: (i, 0)),
                  pl.BlockSpec((tm, LANES), lambda bi, i: (i, 0)),
                  _const_spec((D_MODEL, N_IN))],
        out_specs=[hm(HA_Q), hm(HA_KV), hm(HA_KV), hm(HB), hm(HB), hm(HB), tok, tok],
        out_shape=[hshape(HA_Q), hshape(HA_KV), hshape(HA_KV), hshape(HB), hshape(HB), hshape(HB),
                   jax.ShapeDtypeStruct((b, n, D_MODEL), BF16), jax.ShapeDtypeStruct((b, n, D_MODEL), BF16)],
        compiler_params=_cparams(("parallel", "parallel")),
        name="inproj",
    )(x, mod3, g1, gq, gk, cos_t, sin_t, w_in)


def _stage_scores(s, s_slot, mc_slot):
    s_slot[...] = s
    mc_slot[...] = jnp.broadcast_to(jnp.max(s, axis=1, keepdims=True), mc_slot.shape)


def _stage_update(s_slot, mc_slot, shift, m_ref, accl_ref, v, tk):
    m_prev = m_ref[...]
    if shift is None:
        m_next = jnp.maximum(m_prev, mc_slot[...])
        sub = m_next
    else:
        m_next = jnp.maximum(m_prev, mc_slot[...] + shift)
        sub = m_next - shift
    p = jnp.exp2(s_slot[...] - jnp.concatenate([sub] * (tk // LANES), axis=1)).astype(BF16)
    alpha = jnp.exp2(m_prev - m_next)
    v_ones = jnp.concatenate([v, jnp.ones((tk, LANES), BF16)], axis=1)
    m_ref[...] = m_next
    accl_ref[...] = (accl_ref[...] * jnp.concatenate([alpha, alpha], axis=1)
                     + jnp.dot(p, v_ones, preferred_element_type=F32))


def _pipelined(n_units, scores_fn, update_fn):
    scores_fn(0, 0)
    for u in range(n_units):
        if u + 1 < n_units:
            scores_fn(u + 1, (u + 1) % 2)
        update_fn(u, u % 2)


def _gqa_kernel(q_ref, k_ref, v_ref, sa_ref, o_ref, m_sc, accl_sc, s_sc, mc_sc, *, tq, ts, n_sub, nk):
    ki = pl.program_id(2)

    @pl.when(ki == 0)
    def _():
        m_sc[...] = jnp.full(m_sc.shape, NEG_BIG, F32)
        accl_sc[...] = jnp.zeros(accl_sc.shape, F32)

    def scores(u, slot):
        sub, h = divmod(u, HA_Q)
        k = k_ref[0, h // G_A, sub * ts:(sub + 1) * ts]
        s = lax.dot_general(q_ref[0, h], k, (((1,), (1,)), ((), ())), preferred_element_type=F32)
        _stage_scores(s, s_sc.at[slot], mc_sc.at[slot])

    def update(u, slot):
        sub, h = divmod(u, HA_Q)
        _stage_update(s_sc.at[slot], mc_sc.at[slot], None, m_sc.at[h], accl_sc.at[h],
                      v_ref[0, h // G_A, sub * ts:(sub + 1) * ts], ts)

    _pipelined(n_sub * HA_Q, scores, update)

    @pl.when(ki == nk - 1)
    def _():
        for h in range(HA_Q):
            cols = slice(h * HD_A, (h + 1) * HD_A)
            o = accl_sc[h, :, :HD_A] / accl_sc[h, :, HD_A:]
            o_ref[0, :, cols] = (sa_ref[0, :, cols].astype(F32) * o).astype(BF16)


def _gqa(qa, ka, va, sa):
    b, _, n, _ = qa.shape
    tq, ts, n_sub = TQ_A, TK_A, 1
    tk = ts * n_sub
    nq, nk = n // tq, n // tk
    return pl.pallas_call(
        functools.partial(_gqa_kernel, tq=tq, ts=ts, n_sub=n_sub, nk=nk),
        grid=(b, nq, nk),
        in_specs=[pl.BlockSpec((1, HA_Q, tq, HD_A), lambda bi, qi, ki: (bi, 0, qi, 0)),
                  pl.BlockSpec((1, HA_KV, tk, HD_A), lambda bi, qi, ki: (bi, 0, ki, 0)),
                  pl.BlockSpec((1, HA_KV, tk, HD_A), lambda bi, qi, ki: (bi, 0, ki, 0)),
                  pl.BlockSpec((1, tq, D_MODEL), lambda bi, qi, ki: (bi, qi, 0))],
        out_specs=pl.BlockSpec((1, tq, D_MODEL), lambda bi, qi, ki: (bi, qi, 0)),
        out_shape=jax.ShapeDtypeStruct((b, n, D_MODEL), BF16),
        scratch_shapes=[pltpu.VMEM((HA_Q, tq, LANES), F32),
                        pltpu.VMEM((HA_Q, tq, HD_A + LANES), F32),
                        pltpu.VMEM((2, tq, ts), F32),
                        pltpu.VMEM((2, tq, LANES), F32)],
        compiler_params=_cparams(("parallel", "parallel", "arbitrary")),
        name="gqa",
    )(qa, ka, va, sa)


def _diff_kernel(cb_ref, lam_ref, gs_ref, q_ref, k_ref, v_ref, bias_ref, ga_ref, sb_ref, o_ref,
                 q2_sc, m_sc, accl_sc, s_sc, mc_sc, *, t, nk):
    qi = pl.program_id(1)
    ki = pl.program_id(2)

    @pl.when(ki == 0)
    def _():
        m_sc[...] = jnp.full(m_sc.shape, NEG_BIG, F32)
        accl_sc[...] = jnp.zeros(accl_sc.shape, F32)
        first = lax.broadcasted_iota(jnp.int32, (t, LANES), 1) < DH_B
        zero = jnp.zeros((t, LANES), BF16)
        for h in range(HB):
            q = q_ref[0, h]
            q2_sc[h, :t] = jnp.where(first, q, zero)
            q2_sc[h, t:] = jnp.where(first, zero, q)

    def heads(near):
        def scores(u, slot):
            h, half = divmod(u, 2)
            s = lax.dot_general(q2_sc[h, half * t:(half + 1) * t], k_ref[0, h], (((1,), (1,)), ((), ())),
                                preferred_element_type=F32)
            if near:
                s = s + bias_ref[h, 0]
            _stage_scores(s, s_sc.at[slot], mc_sc.at[slot])

        def update(u, slot):
            h, half = divmod(u, 2)
            rs = slice(half * t, (half + 1) * t)
            shift = None if near else jnp.where(ki > qi, cb_ref[h, 1], cb_ref[h, 0])
            _stage_update(s_sc.at[slot], mc_sc.at[slot], shift, m_sc.at[h, rs], accl_sc.at[h, rs], v_ref[0, h], t)

        _pipelined(2 * HB, scores, update)

    near_diag = jnp.abs(ki - qi) <= 1

    @pl.when(near_diag)
    def _():
        heads(True)

    @pl.when(jnp.logical_not(near_diag))
    def _():
        heads(False)

    @pl.when(ki == nk - 1)
    def _():
        lam = (jnp.exp(jnp.sum(lam_ref[0:1, :] * lam_ref[1:2, :], axis=-1, keepdims=True))
               - jnp.exp(jnp.sum(lam_ref[2:3, :] * lam_ref[3:4, :], axis=-1, keepdims=True))
               + LAMBDA_INIT)
        gs = gs_ref[...]
        for h in range(HB):
            o12 = accl_sc[h, :, :DV_B] / accl_sc[h, :, DV_B:]
            o = o12[:t] - lam * o12[t:]
            o = o * lax.rsqrt(jnp.mean(o * o, axis=-1, keepdims=True) + EPS) * gs * (1.0 - LAMBDA_INIT)
            cols = slice(h * DV_B, (h + 1) * DV_B)
            merged = ga_ref[0, :, cols].astype(F32) + sb_ref[0, :, cols].astype(F32) * o
            o_ref[0, :, cols] = merged.astype(BF16)


def _diff(qb, kb, vb, bias_t, cb, lam4, gs, gated_a, sb):
    b, _, n, _ = qb.shape
    t = T_B
    nq = nk = n // t
    hm = lambda idx: pl.BlockSpec((1, HB, t, LANES), idx)
    return pl.pallas_call(
        functools.partial(_diff_kernel, t=t, nk=nk),
        grid=(b, nq, nk),
        in_specs=[pl.BlockSpec(memory_space=pltpu.SMEM),
                  _const_spec((4, DH_B)), _const_spec((1, DV_B)),
                  hm(lambda bi, qi, ki: (bi, 0, qi, 0)),
                  hm(lambda bi, qi, ki: (bi, 0, ki, 0)),
                  hm(lambda bi, qi, ki: (bi, 0, ki, 0)),
                  pl.BlockSpec((HB, 1, t, t), lambda bi, qi, ki: (0, jnp.clip(ki - qi, -1, 1) + 1, 0, 0)),
                  pl.BlockSpec((1, t, D_MODEL), lambda bi, qi, ki: (bi, qi, 0)),
                  pl.BlockSpec((1, t, D_MODEL), lambda bi, qi, ki: (bi, qi, 0))],
        out_specs=pl.BlockSpec((1, t, D_MODEL), lambda bi, qi, ki: (bi, qi, 0)),
        out_shape=jax.ShapeDtypeStruct((b, n, D_MODEL), BF16),
        scratch_shapes=[pltpu.VMEM((HB, 2 * t, LANES), BF16),
                        pltpu.VMEM((HB, 2 * t, LANES), F32),
                        pltpu.VMEM((HB, 2 * t, DV_B + LANES), F32),
                        pltpu.VMEM((2, t, t), F32),
                        pltpu.VMEM((2, t, LANES), F32)],
        compiler_params=_cparams(("parallel", "parallel", "arbitrary")),
        name="diff",
    )(cb, lam4, gs, qb, kb, vb, bias_t, gated_a, sb)


def _mid_kernel(x_ref, merged_ref, mod_ref, g2_ref, wo_ref, x1_ref, h2_ref):
    merged = merged_ref[0]
    gt1 = mod_ref[0, 2:3, :]
    sh2 = mod_ref[0, 3:4, :]
    sc2 = mod_ref[0, 4:5, :]
    x1 = x_ref[0] + gt1 * jnp.dot(merged, wo_ref[...], preferred_element_type=F32)
    x1_ref[0] = x1
    y = x1 * lax.rsqrt(jnp.mean(x1 * x1, axis=-1, keepdims=True) + EPS)
    h2_ref[0] = ((y * g2_ref[...]) * (1.0 + sc2) + sh2).astype(BF16)


def _mid(x, merged, mod3, g2, w_out):
    b, n, _ = x.shape
    tm = TM_MID
    tok = pl.BlockSpec((1, tm, D_MODEL), lambda bi, i: (bi, i, 0))
    return pl.pallas_call(
        _mid_kernel,
        grid=(b, n // tm),
        in_specs=[tok, tok,
                  pl.BlockSpec((1, 6, D_MODEL), lambda bi, i: (bi, 0, 0)),
                  _const_spec((1, D_MODEL)),
                  _const_spec((D_MODEL, D_MODEL))],
        out_specs=[tok, tok],
        out_shape=[jax.ShapeDtypeStruct((b, n, D_MODEL), F32),
                   jax.ShapeDtypeStruct((b, n, D_MODEL), BF16)],
        compiler_params=_cparams(("parallel", "parallel")),
        name="mid",
    )(x, merged, mod3, g2, w_out)


def _ffn_kernel(x1_ref, h_ref, hp_ref, hn_ref, mod_ref, wf_ref, cw_ref, cbias_ref, gf_ref, wd_ref,
                y_ref, gs_sc, us_sc, *, tm, nt):
    i = pl.program_id(1)
    rows = tm + 2 * HALO
    h2 = jnp.concatenate([hp_ref[0], h_ref[0], hn_ref[0]], axis=0)
    h2m = h_ref[0]
    ridx = lax.broadcasted_iota(jnp.int32, (rows, 1), 0)
    keep = jnp.logical_and(jnp.logical_or(i > 0, ridx >= HALO), jnp.logical_or(i < nt - 1, ridx < HALO + tm))
    n_chunks = D_FF // FF_CHUNK

    def ffn_in(c, slot):
        cs = slice(c * FF_CHUNK, (c + 1) * FF_CHUNK)
        gcs = slice(D_FF + c * FF_CHUNK, D_FF + (c + 1) * FF_CHUNK)
        us_sc[slot] = jnp.dot(h2m, wf_ref[:, cs], preferred_element_type=F32)
        gs_sc[slot] = jnp.where(keep, jnp.dot(h2, wf_ref[:, gcs], preferred_element_type=F32), 0.0)

    def ffn_out(c, slot, acc):
        cs = slice(c * FF_CHUNK, (c + 1) * FF_CHUNK)
        z = (cbias_ref[:, cs] + gs_sc[slot, HALO - 1:HALO - 1 + tm, :] * cw_ref[0:1, cs]
             + gs_sc[slot, HALO:HALO + tm, :] * cw_ref[1:2, cs]
             + gs_sc[slot, HALO + 1:HALO + 1 + tm, :] * cw_ref[2:3, cs])
        a = (z * (1.0 + lax.erf(z * math.sqrt(0.5))) * us_sc[slot]).astype(BF16)
        d = jnp.dot(a, wd_ref[cs, :], preferred_element_type=F32)
        return d if acc is None else acc + d

    ffn_in(0, 0)
    acc = None
    for c in range(n_chunks):
        if c + 1 < n_chunks:
            ffn_in(c + 1, (c + 1) % 2)
        acc = ffn_out(c, c % 2, acc)
    gt2 = mod_ref[0, 5:6, :]
    x2 = x1_ref[0] + gt2 * acc
    y_ref[0] = x2 * lax.rsqrt(jnp.mean(x2 * x2, axis=-1, keepdims=True) + EPS) * gf_ref[...]


def _ffn(x1, h2, mod3, w_ffn_in, conv_w, conv_b, g_final, w_down_half):
    b, n, _ = x1.shape
    tm = TM_FFN
    nt = n // tm
    rb = tm // HALO
    tok = pl.BlockSpec((1, tm, D_MODEL), lambda bi, i: (bi, i, 0))
    prv = pl.BlockSpec((1, HALO, D_MODEL), lambda bi, i: (bi, jnp.maximum(i * rb - 1, 0), 0))
    nxt = pl.BlockSpec((1, HALO, D_MODEL), lambda bi, i: (bi, jnp.minimum((i + 1) * rb, n // HALO - 1), 0))
    return pl.pallas_call(
        functools.partial(_ffn_kernel, tm=tm, nt=nt),
        grid=(b, nt),
        in_specs=[tok, tok, prv, nxt,
                  pl.BlockSpec((1, 6, D_MODEL), lambda bi, i: (bi, 0, 0)),
                  _const_spec((D_MODEL, 2 * D_FF)),
                  _const_spec((CONV_W, D_FF)), _const_spec((1, D_FF)),
                  _const_spec((1, D_MODEL)),
                  _const_spec((D_FF, D_MODEL))],
        out_specs=tok,
        out_shape=jax.ShapeDtypeStruct((b, n, D_MODEL), F32),
        scratch_shapes=[pltpu.VMEM((2, tm + 2 * HALO, FF_CHUNK), F32),
                        pltpu.VMEM((2, tm, FF_CHUNK), F32)],
        compiler_params=_cparams(("parallel", "parallel")),
        name="ffn",
    )(x1, h2, h2, h2, mod3, w_ffn_in, conv_w, conv_b, g_final, w_down_half)


def _rope_tables(n):
    t = jnp.arange(n, dtype=jnp.int32)
    row = (t // GRID_W).astype(F32)
    col = (t % GRID_W).astype(F32)
    inv = ROPE_THETA ** (-jnp.arange(0, ROPE_AXIS_DIM, 2, dtype=F32) / ROPE_AXIS_DIM)
    ar = row[:, None] * inv[None, :]
    ac = col[:, None] * inv[None, :]
    cos_t = jnp.concatenate([jnp.cos(ar), jnp.cos(ar), jnp.cos(ac), jnp.cos(ac)], axis=-1)
    sin_t = jnp.concatenate([-jnp.sin(ar), jnp.sin(ar), -jnp.sin(ac), jnp.sin(ac)], axis=-1)
    return cos_t, sin_t


def _trunk(x, mod, p):
    b, n, _ = x.shape
    mod3 = mod.reshape(b, 6, D_MODEL)
    cos_t, sin_t = p['rope']
    qa, ka, va, qb, kb, vb, ga, gb = _inproj(x, mod3, p['g_norm1'], p['g_qnorm'], p['g_knorm'],
                                             cos_t, sin_t, p['w_in'])
    gated_a = _gqa(qa, ka, va, ga)
    merged = _diff(qb, kb, vb, p['bias_t'], p['cb'], p['lam4'], p['g_subln'], gated_a, gb)
    x1, h2 = _mid(x, merged, mod3, p['g_norm2'], p['w_out'])
    return _ffn(x1, h2, mod3, p['w_ffn_in'], p['conv_w'], p['conv_b'], p['g_final'], p['w_down'])


def kernel(x_prompt, x_sample, c_prompt, c_sample, w_mod, b_mod, g_norm1, w_in, g_qnorm, g_knorm,
           lambda_q1, lambda_k1, lambda_q2, lambda_k2, g_subln, rel_bias, w_out, g_norm2,
           w_ffn_in, conv_w, conv_b, w_down, g_final):
    bp, bs = x_prompt.shape[0], x_sample.shape[0]
    rows = bp + bs
    pad = (-rows) % 8
    c_all = jnp.pad(jnp.concatenate([c_prompt, c_sample], axis=0), ((0, pad), (0, 0)))
    mod = _mod(c_all, w_mod[0], b_mod)

    p = {
        'g_norm1': g_norm1, 'g_qnorm': g_qnorm, 'g_knorm': g_knorm, 'g_subln': g_subln,
        'g_norm2': g_norm2, 'g_final': g_final.reshape(1, D_MODEL),
        'w_in': w_in[0].astype(BF16), 'w_out': w_out[0].astype(BF16),
        'w_ffn_in': w_ffn_in[0].astype(BF16), 'w_down': (w_down[0] * 0.5).astype(BF16),
        'conv_w': conv_w[0], 'conv_b': conv_b,
        'lam4': jnp.concatenate([lambda_q1, lambda_k1, lambda_q2, lambda_k2], axis=0),
        'bias_t': _bias_tiles(rel_bias, T_B),
        'rope': _rope_tables(max(x_prompt.shape[1], x_sample.shape[1])),
        'cb': jnp.stack([rel_bias[N_BUCKETS // 2 - 1], rel_bias[N_BUCKETS - 1]], axis=1) * LOG2E,
    }
    y_prompt = _trunk(x_prompt, mod[:bp], p)
    y_sample = _trunk(x_sample, mod[bp:rows], p)
    return (y_prompt, y_sample)
```

```python
import functools
import math

import jax
import jax.numpy as jnp
from jax import lax
from jax.experimental import pallas as pl
from jax.experimental.pallas import tpu as pltpu

F32 = jnp.float32
BF16 = jnp.bfloat16

D_MODEL = 1024
GRID_W = 64
EPS = 1e-6
HA_Q, HA_KV, G_A, HD_A = 8, 2, 4, 128
ROPE_AXIS_DIM = HD_A // 2
ROPE_THETA = 10000.0
HB, DH_B, DV_B = 8, 64, 128
N_BUCKETS, MAX_DIST = 32, 128
D_FF = 2816
CONV_W = 3
LAMBDA_INIT = 0.8 - 0.6 * math.exp(-0.3 * 0)
LOG2E = 1.4426950408889634
SCALE_A = HD_A ** -0.5 * LOG2E
SCALE_B = DH_B ** -0.5 * LOG2E
NEG_BIG = -0.7 * float(jnp.finfo(jnp.float32).max)

OFF_QA, OFF_KA, OFF_VA = 0, 1024, 1280
OFF_QB, OFF_KB, OFF_VB = 1536, 2560, 3584
OFF_GA, OFF_GB = 4608, 5632
N_IN = 6656

LANES = 128
VMEM_LIMIT = 56 * 1024 * 1024

TM_IN = 512
PROJ_CHUNK = 512
TM_MID = 512
TM_FFN = 512
HALO = 16
FF_CHUNK = 256
TQ_A = 1024
TK_A = 2048
MOD_BLK = 1024
ROPE_SWAP = ROPE_AXIS_DIM // 2
T_B = 512


def _cparams(sem):
    return pltpu.CompilerParams(dimension_semantics=sem, vmem_limit_bytes=VMEM_LIMIT)


def _const_spec(shape):
    nd = len(shape)
    return pl.BlockSpec(shape, lambda *_: (0,) * nd, pipeline_mode=pl.Buffered(1))


def _mod_kernel(c_ref, w_ref, b_ref, o_ref):
    c = c_ref[...]
    sc = c * jax.nn.sigmoid(c)
    o_ref[...] = jnp.dot(sc, w_ref[...], preferred_element_type=F32,
                         precision=lax.Precision.HIGHEST) + b_ref[...]


def _mod(c, w_mod, b_mod):
    rows = c.shape[0]
    n_out = w_mod.shape[1]
    blk = MOD_BLK
    return pl.pallas_call(
        _mod_kernel,
        grid=(n_out // blk,),
        in_specs=[pl.BlockSpec((rows, D_MODEL), lambda j: (0, 0)),
                  pl.BlockSpec((D_MODEL, blk), lambda j: (0, j)),
                  pl.BlockSpec((1, blk), lambda j: (0, j))],
        out_specs=pl.BlockSpec((rows, blk), lambda j: (0, j)),
        out_shape=jax.ShapeDtypeStruct((rows, n_out), F32),
        compiler_params=_cparams(("parallel",)),
        name="mod",
    )(c, w_mod, b_mod)


def _bias_kernel(tab_ref, o_ref, *, t):
    cls = pl.program_id(0)
    row = lax.broadcasted_iota(jnp.int32, (t, t), 0)
    col = lax.broadcasted_iota(jnp.int32, (t, t), 1)
    rel = (cls - 1) * t + col - row
    nb = N_BUCKETS // 2
    max_exact = nb // 2
    ret = jnp.where(rel > 0, nb, 0)
    n = jnp.abs(rel)
    large = max_exact + (jnp.log(jnp.maximum(n, 1).astype(F32) / max_exact)
                         / math.log(MAX_DIST / max_exact) * (nb - max_exact)).astype(jnp.int32)
    large = jnp.minimum(large, nb - 1)
    bucket = ret + jnp.where(n < max_exact, n, large)
    for h in range(HB):
        acc = jnp.zeros((t, t), F32)
        for b in range(N_BUCKETS):
            acc = jnp.where(bucket == b, tab_ref[b, h], acc)
        o_ref[h, 0] = acc * LOG2E


def _bias_tiles(rel_bias, t):
    return pl.pallas_call(
        functools.partial(_bias_kernel, t=t),
        grid=(3,),
        in_specs=[pl.BlockSpec(memory_space=pltpu.SMEM)],
        out_specs=pl.BlockSpec((HB, 1, t, t), lambda c: (0, c, 0, 0)),
        out_shape=jax.ShapeDtypeStruct((HB, 3, t, t), F32),
        compiler_params=_cparams(("parallel",)),
        name="bias_tiles",
    )(rel_bias)


def _inproj_kernel(x_ref, mod_ref, g1_ref, gq_ref, gk_ref, cos_ref, sin_ref, w_ref,
                   qa_ref, ka_ref, va_ref, qb_ref, kb_ref, vb_ref, ga_ref, gb_ref, *, tm):
    x = x_ref[0]
    y = x * lax.rsqrt(jnp.mean(x * x, axis=-1, keepdims=True) + EPS)
    sh1 = mod_ref[0, 0:1, :]
    sc1 = mod_ref[0, 1:2, :]
    hb = ((y * g1_ref[...]) * (1.0 + sc1) + sh1).astype(BF16)

    cos = cos_ref[...]
    sin = sin_ref[...]
    lane = lax.broadcasted_iota(jnp.int32, (tm, LANES), 1)
    low_half = (lane & ROPE_SWAP) == 0

    def proj(c0, width):
        return jnp.dot(hb, w_ref[:, c0:c0 + width], preferred_element_type=F32)

    def norm_rope(p, g, scale):
        yn = p * lax.rsqrt(jnp.mean(p * p, axis=-1, keepdims=True) + EPS) * g
        partner = jnp.where(low_half, pltpu.roll(yn, LANES - ROPE_SWAP, 1), pltpu.roll(yn, ROPE_SWAP, 1))
        r = yn * cos + partner * sin
        return r * scale if scale is not None else r

    gq = gq_ref[...]
    gk = gk_ref[...]
    for c in range(2):
        ga_ref[0, :, c * PROJ_CHUNK:(c + 1) * PROJ_CHUNK] = jax.nn.sigmoid(proj(OFF_GA + c * PROJ_CHUNK, PROJ_CHUNK)).astype(BF16)
    for c in range(2):
        gb_ref[0, :, c * PROJ_CHUNK:(c + 1) * PROJ_CHUNK] = jax.nn.sigmoid(proj(OFF_GB + c * PROJ_CHUNK, PROJ_CHUNK)).astype(BF16)
    for c in range(2):
        p = proj(OFF_QA + c * PROJ_CHUNK, PROJ_CHUNK)
        for j in range(4):
            qa_ref[0, c * 4 + j] = norm_rope(p[:, j * LANES:(j + 1) * LANES], gq, SCALE_A).astype(BF16)
    p = proj(OFF_KA, PROJ_CHUNK)
    for j in range(2):
        ka_ref[0, j] = norm_rope(p[:, j * LANES:(j + 1) * LANES], gk, None).astype(BF16)
        va_ref[0, j] = p[:, (2 + j) * LANES:(3 + j) * LANES].astype(BF16)
    for c in range(2):
        p = proj(OFF_QB + c * PROJ_CHUNK, PROJ_CHUNK)
        for j in range(4):
            qb_ref[0, c * 4 + j] = (p[:, j * LANES:(j + 1) * LANES] * SCALE_B).astype(BF16)
    for c in range(2):
        p = proj(OFF_KB + c * PROJ_CHUNK, PROJ_CHUNK)
        for j in range(4):
            kb_ref[0, c * 4 + j] = p[:, j * LANES:(j + 1) * LANES].astype(BF16)
    for c in range(2):
        p = proj(OFF_VB + c * PROJ_CHUNK, PROJ_CHUNK)
        for j in range(4):
            vb_ref[0, c * 4 + j] = p[:, j * LANES:(j + 1) * LANES].astype(BF16)


def _inproj(x, mod3, g1, gq, gk, cos_t, sin_t, w_in):
    b, n, _ = x.shape
    tm = TM_IN
    hm = lambda heads: pl.BlockSpec((1, heads, tm, LANES), lambda bi, i: (bi, 0, i, 0))
    tok = pl.BlockSpec((1, tm, D_MODEL), lambda bi, i: (bi, i, 0))
    hshape = lambda heads: jax.ShapeDtypeStruct((b, heads, n, LANES), BF16)
    return pl.pallas_call(
        functools.partial(_inproj_kernel, tm=tm),
        grid=(b, n // tm),
        in_specs=[tok,
                  pl.BlockSpec((1, 6, D_MODEL), lambda bi, i: (bi, 0, 0)),
                  _const_spec((1, D_MODEL)), _const_spec((1, LANES)), _const_spec((1, LANES)),
                  pl.BlockSpec((tm, LANES), lambda bi, i: (i, 0)),
                  pl.BlockSpec((tm, LANES), lambda bi, i: (i, 0)),
                  _const_spec((D_MODEL, N_IN))],
        out_specs=[hm(HA_Q), hm(HA_KV), hm(HA_KV), hm(HB), hm(HB), hm(HB), tok, tok],
        out_shape=[hshape(HA_Q), hshape(HA_KV), hshape(HA_KV), hshape(HB), hshape(HB), hshape(HB),
                   jax.ShapeDtypeStruct((b, n, D_MODEL), BF16), jax.ShapeDtypeStruct((b, n, D_MODEL), BF16)],
        compiler_params=_cparams(("parallel", "parallel")),
        name="inproj",
    )(x, mod3, g1, gq, gk, cos_t, sin_t, w_in)


def _stage_scores(s, s_slot, mc_slot):
    s_slot[...] = s
    mc_slot[...] = jnp.broadcast_to(jnp.max(s, axis=1, keepdims=True), mc_slot.shape)


def _stage_update(s_slot, mc_slot, shift, m_ref, accl_ref, v, tk, first=None):
    m_prev = m_ref[...]
    accl_prev = accl_ref[...]
    if first is not None:
        m_prev = jnp.where(first, NEG_BIG, m_prev)
        accl_prev = jnp.where(first, 0.0, accl_prev)
    if shift is None:
        m_next = jnp.maximum(m_prev, mc_slot[...])
        sub = m_next
    else:
        m_next = jnp.maximum(m_prev, mc_slot[...] + shift)
        sub = m_next - shift
    p = jnp.exp2(s_slot[...] - jnp.concatenate([sub] * (tk // LANES), axis=1)).astype(BF16)
    alpha = jnp.exp2(m_prev - m_next)
    v_ones = jnp.concatenate([v, jnp.ones((tk, LANES), BF16)], axis=1)
    m_ref[...] = m_next
    accl_ref[...] = (accl_prev * jnp.concatenate([alpha, alpha], axis=1)
                     + jnp.dot(p, v_ones, preferred_element_type=F32))


def _pipelined(n_units, scores_fn, update_fn):
    scores_fn(0, 0)
    for u in range(n_units):
        if u + 1 < n_units:
            scores_fn(u + 1, (u + 1) % 2)
        update_fn(u, u % 2)


def _gqa_kernel(q_ref, k_ref, v_ref, sa_ref, o_ref, m_sc, accl_sc, s_sc, mc_sc, *, tq, ts, n_sub, nk):
    ki = pl.program_id(2)

    @pl.when(jnp.logical_and(pl.program_id(0) == 0, jnp.logical_and(pl.program_id(1) == 0, ki == 0)))
    def _():
        m_sc[...] = jnp.full(m_sc.shape, NEG_BIG, F32)
        accl_sc[...] = jnp.zeros(accl_sc.shape, F32)

    def scores(u, slot):
        sub, h = divmod(u, HA_Q)
        k = k_ref[0, h // G_A, sub * ts:(sub + 1) * ts]
        s = lax.dot_general(q_ref[0, h], k, (((1,), (1,)), ((), ())), preferred_element_type=F32)
        _stage_scores(s, s_sc.at[slot], mc_sc.at[slot])

    def update(u, slot):
        sub, h = divmod(u, HA_Q)
        _stage_update(s_sc.at[slot], mc_sc.at[slot], None, m_sc.at[h], accl_sc.at[h],
                      v_ref[0, h // G_A, sub * ts:(sub + 1) * ts], ts,
                      first=(ki == 0) if sub == 0 else None)

    _pipelined(n_sub * HA_Q, scores, update)

    @pl.when(ki == nk - 1)
    def _():
        for h in range(HA_Q):
            cols = slice(h * HD_A, (h + 1) * HD_A)
            o = accl_sc[h, :, :HD_A] / accl_sc[h, :, HD_A:]
            o_ref[0, :, cols] = (sa_ref[0, :, cols].astype(F32) * o).astype(BF16)


def _gqa(qa, ka, va, sa):
    b, _, n, _ = qa.shape
    tq, ts, n_sub = TQ_A, TK_A, 1
    tk = ts * n_sub
    nq, nk = n // tq, n // tk
    return pl.pallas_call(
        functools.partial(_gqa_kernel, tq=tq, ts=ts, n_sub=n_sub, nk=nk),
        grid=(b, nq, nk),
        in_specs=[pl.BlockSpec((1, HA_Q, tq, HD_A), lambda bi, qi, ki: (bi, 0, qi, 0)),
                  pl.BlockSpec((1, HA_KV, tk, HD_A), lambda bi, qi, ki: (bi, 0, ki, 0)),
                  pl.BlockSpec((1, HA_KV, tk, HD_A), lambda bi, qi, ki: (bi, 0, ki, 0)),
                  pl.BlockSpec((1, tq, D_MODEL), lambda bi, qi, ki: (bi, qi, 0))],
        out_specs=pl.BlockSpec((1, tq, D_MODEL), lambda bi, qi, ki: (bi, qi, 0)),
        out_shape=jax.ShapeDtypeStruct((b, n, D_MODEL), BF16),
        scratch_shapes=[pltpu.VMEM((HA_Q, tq, LANES), F32),
                        pltpu.VMEM((HA_Q, tq, HD_A + LANES), F32),
                        pltpu.VMEM((2, tq, ts), F32),
                        pltpu.VMEM((2, tq, LANES), F32)],
        compiler_params=_cparams(("arbitrary", "arbitrary", "arbitrary")),
        name="gqa",
    )(qa, ka, va, sa)


def _diff_kernel(cb_ref, lam_ref, gs_ref, q_ref, k_ref, v_ref, bias_ref, ga_ref, sb_ref, o_ref,
                 q2_sc, m_sc, accl_sc, s_sc, mc_sc, *, t, nk):
    qi = pl.program_id(1)
    ki = pl.program_id(2)

    @pl.when(ki == 0)
    def _():
        m_sc[...] = jnp.full(m_sc.shape, NEG_BIG, F32)
        accl_sc[...] = jnp.zeros(accl_sc.shape, F32)
        first = lax.broadcasted_iota(jnp.int32, (t, LANES), 1) < DH_B
        zero = jnp.zeros((t, LANES), BF16)
        for h in range(HB):
            q = q_ref[0, h]
            q2_sc[h, :t] = jnp.where(first, q, zero)
            q2_sc[h, t:] = jnp.where(first, zero, q)

    def heads(near):
        def scores(u, slot):
            h, half = divmod(u, 2)
            s = lax.dot_general(q2_sc[h, half * t:(half + 1) * t], k_ref[0, h], (((1,), (1,)), ((), ())),
                                preferred_element_type=F32)
            if near:
                s = s + bias_ref[h, 0]
            _stage_scores(s, s_sc.at[slot], mc_sc.at[slot])

        def update(u, slot):
            h, half = divmod(u, 2)
            rs = slice(half * t, (half + 1) * t)
            shift = None if near else jnp.where(ki > qi, cb_ref[h, 1], cb_ref[h, 0])
            _stage_update(s_sc.at[slot], mc_sc.at[slot], shift, m_sc.at[h, rs], accl_sc.at[h, rs], v_ref[0, h], t)

        _pipelined(2 * HB, scores, update)

    near_diag = jnp.abs(ki - qi) <= 1

    @pl.when(near_diag)
    def _():
        heads(True)

    @pl.when(jnp.logical_not(near_diag))
    def _():
        heads(False)

    @pl.when(ki == nk - 1)
    def _():
        lam = (jnp.exp(jnp.sum(lam_ref[0:1, :] * lam_ref[1:2, :], axis=-1, keepdims=True))
               - jnp.exp(jnp.sum(lam_ref[2:3, :] * lam_ref[3:4, :], axis=-1, keepdims=True))
               + LAMBDA_INIT)
        gs = gs_ref[...]
        for h in range(HB):
            o12 = accl_sc[h, :, :DV_B] / accl_sc[h, :, DV_B:]
            o = o12[:t] - lam * o12[t:]
            o = o * lax.rsqrt(jnp.mean(o * o, axis=-1, keepdims=True) + EPS) * gs * (1.0 - LAMBDA_INIT)
            cols = slice(h * DV_B, (h + 1) * DV_B)
            merged = ga_ref[0, :, cols].astype(F32) + sb_ref[0, :, cols].astype(F32) * o
            o_ref[0, :, cols] = merged.astype(BF16)


def _diff(qb, kb, vb, bias_t, cb, lam4, gs, gated_a, sb):
    b, _, n, _ = qb.shape
    t = T_B
    nq = nk = n // t
    hm = lambda idx: pl.BlockSpec((1, HB, t, LANES), idx)
    return pl.pallas_call(
        functools.partial(_diff_kernel, t=t, nk=nk),
        grid=(b, nq, nk),
        in_specs=[pl.BlockSpec(memory_space=pltpu.SMEM),
                  _const_spec((4, DH_B)), _const_spec((1, DV_B)),
                  hm(lambda bi, qi, ki: (bi, 0, qi, 0)),
                  hm(lambda bi, qi, ki: (bi, 0, ki, 0)),
                  hm(lambda bi, qi, ki: (bi, 0, ki, 0)),
                  pl.BlockSpec((HB, 1, t, t), lambda bi, qi, ki: (0, jnp.clip(ki - qi, -1, 1) + 1, 0, 0)),
                  pl.BlockSpec((1, t, D_MODEL), lambda bi, qi, ki: (bi, qi, 0)),
                  pl.BlockSpec((1, t, D_MODEL), lambda bi, qi, ki: (bi, qi, 0))],
        out_specs=pl.BlockSpec((1, t, D_MODEL), lambda bi, qi, ki: (bi, qi, 0)),
        out_shape=jax.ShapeDtypeStruct((b, n, D_MODEL), BF16),
        scratch_shapes=[pltpu.VMEM((HB, 2 * t, LANES), BF16),
                        pltpu.VMEM((HB, 2 * t, LANES), F32),
                        pltpu.VMEM((HB, 2 * t, DV_B + LANES), F32),
                        pltpu.VMEM((2, t, t), F32),
                        pltpu.VMEM((2, t, LANES), F32)],
        compiler_params=_cparams(("parallel", "parallel", "arbitrary")),
        name="diff",
    )(cb, lam4, gs, qb, kb, vb, bias_t, gated_a, sb)


def _mid_kernel(x_ref, merged_ref, mod_ref, g2_ref, wo_ref, x1_ref, h2_ref):
    merged = merged_ref[0]
    gt1 = mod_ref[0, 2:3, :]
    sh2 = mod_ref[0, 3:4, :]
    sc2 = mod_ref[0, 4:5, :]
    x1 = x_ref[0] + gt1 * jnp.dot(merged, wo_ref[...], preferred_element_type=F32)
    x1_ref[0] = x1
    y = x1 * lax.rsqrt(jnp.mean(x1 * x1, axis=-1, keepdims=True) + EPS)
    h2_ref[0] = ((y * g2_ref[...]) * (1.0 + sc2) + sh2).astype(BF16)


def _mid(x, merged, mod3, g2, w_out):
    b, n, _ = x.shape
    tm = TM_MID
    tok = pl.BlockSpec((1, tm, D_MODEL), lambda bi, i: (bi, i, 0))
    return pl.pallas_call(
        _mid_kernel,
        grid=(b, n // tm),
        in_specs=[tok, tok,
                  pl.BlockSpec((1, 6, D_MODEL), lambda bi, i: (bi, 0, 0)),
                  _const_spec((1, D_MODEL)),
                  _const_spec((D_MODEL, D_MODEL))],
        out_specs=[tok, tok],
        out_shape=[jax.ShapeDtypeStruct((b, n, D_MODEL), F32),
                   jax.ShapeDtypeStruct((b, n, D_MODEL), BF16)],
        compiler_params=_cparams(("parallel", "parallel")),
        name="mid",
    )(x, merged, mod3, g2, w_out)


def _ffn_kernel(x1_ref, h_ref, hp_ref, hn_ref, mod_ref, wf_ref, cw_ref, cbias_ref, gf_ref, wd_ref,
                y_ref, gs_sc, us_sc, *, tm, nt):
    i = pl.program_id(1)
    rows = tm + 2 * HALO
    h2 = jnp.concatenate([hp_ref[0], h_ref[0], hn_ref[0]], axis=0)
    h2m = h_ref[0]
    ridx = lax.broadcasted_iota(jnp.int32, (rows, 1), 0)
    keep = jnp.logical_and(jnp.logical_or(i > 0, ridx >= HALO), jnp.logical_or(i < nt - 1, ridx < HALO + tm))
    n_chunks = D_FF // FF_CHUNK

    def ffn_in(c, slot):
        cs = slice(c * FF_CHUNK, (c + 1) * FF_CHUNK)
        gcs = slice(D_FF + c * FF_CHUNK, D_FF + (c + 1) * FF_CHUNK)
        us_sc[slot] = jnp.dot(h2m, wf_ref[:, cs], preferred_element_type=F32)
        gs_sc[slot] = jnp.where(keep, jnp.dot(h2, wf_ref[:, gcs], preferred_element_type=F32), 0.0)

    def ffn_out(c, slot, acc):
        cs = slice(c * FF_CHUNK, (c + 1) * FF_CHUNK)
        z = (cbias_ref[:, cs] + gs_sc[slot, HALO - 1:HALO - 1 + tm, :] * cw_ref[0:1, cs]
             + gs_sc[slot, HALO:HALO + tm, :] * cw_ref[1:2, cs]
             + gs_sc[slot, HALO + 1:HALO + 1 + tm, :] * cw_ref[2:3, cs])
        a = (z * (1.0 + lax.erf(z * math.sqrt(0.5))) * us_sc[slot]).astype(BF16)
        d = jnp.dot(a, wd_ref[cs, :], preferred_element_type=F32)
        return d if acc is None else acc + d

    ffn_in(0, 0)
    acc = None
    for c in range(n_chunks):
        if c + 1 < n_chunks:
            ffn_in(c + 1, (c + 1) % 2)
        acc = ffn_out(c, c % 2, acc)
    gt2 = mod_ref[0, 5:6, :]
    x2 = x1_ref[0] + gt2 * acc
    y_ref[0] = x2 * lax.rsqrt(jnp.mean(x2 * x2, axis=-1, keepdims=True) + EPS) * gf_ref[...]


def _ffn(x1, h2, mod3, w_ffn_in, conv_w, conv_b, g_final, w_down_half):
    b, n, _ = x1.shape
    tm = TM_FFN
    nt = n // tm
    rb = tm // HALO
    tok = pl.BlockSpec((1, tm, D_MODEL), lambda bi, i: (bi, i, 0))
    prv = pl.BlockSpec((1, HALO, D_MODEL), lambda bi, i: (bi, jnp.maximum(i * rb - 1, 0), 0))
    nxt = pl.BlockSpec((1, HALO, D_MODEL), lambda bi, i: (bi, jnp.minimum((i + 1) * rb, n // HALO - 1), 0))
    return pl.pallas_call(
        functools.partial(_ffn_kernel, tm=tm, nt=nt),
        grid=(b, nt),
        in_specs=[tok, tok, prv, nxt,
                  pl.BlockSpec((1, 6, D_MODEL), lambda bi, i: (bi, 0, 0)),
                  _const_spec((D_MODEL, 2 * D_FF)),
                  _const_spec((CONV_W, D_FF)), _const_spec((1, D_FF)),
                  _const_spec((1, D_MODEL)),
                  _const_spec((D_FF, D_MODEL))],
        out_specs=tok,
        out_shape=jax.ShapeDtypeStruct((b, n, D_MODEL), F32),
        scratch_shapes=[pltpu.VMEM((2, tm + 2 * HALO, FF_CHUNK), F32),
                        pltpu.VMEM((2, tm, FF_CHUNK), F32)],
        compiler_params=_cparams(("parallel", "parallel")),
        name="ffn",
    )(x1, h2, h2, h2, mod3, w_ffn_in, conv_w, conv_b, g_final, w_down_half)


def _rope_tables(n):
    t = jnp.arange(n, dtype=jnp.int32)
    row = (t // GRID_W).astype(F32)
    col = (t % GRID_W).astype(F32)
    inv = ROPE_THETA ** (-jnp.arange(0, ROPE_AXIS_DIM, 2, dtype=F32) / ROPE_AXIS_DIM)
    ar = row[:, None] * inv[None, :]
    ac = col[:, None] * inv[None, :]
    cos_t = jnp.concatenate([jnp.cos(ar), jnp.cos(ar), jnp.cos(ac), jnp.cos(ac)], axis=-1)
    sin_t = jnp.concatenate([-jnp.sin(ar), jnp.sin(ar), -jnp.sin(ac), jnp.sin(ac)], axis=-1)
    return cos_t, sin_t


def _trunk(x, mod, p):
    b, n, _ = x.shape
    mod3 = mod.reshape(b, 6, D_MODEL)
    cos_t, sin_t = p['rope']
    qa, ka, va, qb, kb, vb, ga, gb = _inproj(x, mod3, p['g_norm1'], p['g_qnorm'], p['g_knorm'],
                                             cos_t, sin_t, p['w_in'])
    gated_a = _gqa(qa, ka, va, ga)
    merged = _diff(qb, kb, vb, p['bias_t'], p['cb'], p['lam4'], p['g_subln'], gated_a, gb)
    x1, h2 = _mid(x, merged, mod3, p['g_norm2'], p['w_out'])
    return _ffn(x1, h2, mod3, p['w_ffn_in'], p['conv_w'], p['conv_b'], p['g_final'], p['w_down'])


def kernel(x_prompt, x_sample, c_prompt, c_sample, w_mod, b_mod, g_norm1, w_in, g_qnorm, g_knorm,
           lambda_q1, lambda_k1, lambda_q2, lambda_k2, g_subln, rel_bias, w_out, g_norm2,
           w_ffn_in, conv_w, conv_b, w_down, g_final):
    bp, bs = x_prompt.shape[0], x_sample.shape[0]
    rows = bp + bs
    pad = (-rows) % 8
    c_all = jnp.pad(jnp.concatenate([c_prompt, c_sample], axis=0), ((0, pad), (0, 0)))
    mod = _mod(c_all, w_mod[0], b_mod)

    p = {
        'g_norm1': g_norm1, 'g_qnorm': g_qnorm, 'g_knorm': g_knorm, 'g_subln': g_subln,
        'g_norm2': g_norm2, 'g_final': g_final.reshape(1, D_MODEL),
        'w_in': w_in[0].astype(BF16), 'w_out': w_out[0].astype(BF16),
        'w_ffn_in': w_ffn_in[0].astype(BF16), 'w_down': (w_down[0] * 0.5).astype(BF16),
        'conv_w': conv_w[0], 'conv_b': conv_b,
        'lam4': jnp.concatenate([lambda_q1, lambda_k1, lambda_q2, lambda_k2], axis=0),
        'bias_t': _bias_tiles(rel_bias, T_B),
        'rope': _rope_tables(max(x_prompt.shape[1], x_sample.shape[1])),
        'cb': jnp.stack([rel_bias[N_BUCKETS // 2 - 1], rel_bias[N_BUCKETS - 1]], axis=1) * LOG2E,
    }
    y_prompt = _trunk(x_prompt, mod[:bp], p)
    y_sample = _trunk(x_sample, mod[bp:rows], p)
    return (y_prompt, y_sample)
```
